```python
import math
import jax, jax.numpy as jnp
from jax import lax
import numpy as np

D_MODEL = 1024
BATCH = 4
SEQ = 8192
DEPTH = 2

RWKV_WIDTH = D_MODEL // 2
RWKV_HEAD_DIM = 64
RWKV_HEADS = RWKV_WIDTH // RWKV_HEAD_DIM
W_LORA = 64
A_LORA = 64
G_LORA = 128
RWKV_PROJ = 3 * RWKV_WIDTH + W_LORA + A_LORA + G_LORA
RWKV_NORM_EPS = 64e-5
POOL_WINDOWS = (2, 4, 8, 16)
POOL_GROUPS = len(POOL_WINDOWS)
POOL_WIDTH = D_MODEL // 2
POOL_GROUP_DIM = POOL_WIDTH // POOL_GROUPS
EVEN_PROJ = RWKV_PROJ + POOL_WIDTH
CONV_CHANNELS = D_MODEL // 2
CONV_WIDTH = 31
RET_WIDTH = D_MODEL // 2
RET_HEAD_DIM = 64
RET_HEADS = RET_WIDTH // RET_HEAD_DIM
RET_CHUNK = 128
ROPE_BASE = 10000.0
ODD_PROJ = 2 * CONV_CHANNELS + 4 * RET_WIDTH
MIX_WIDTH = D_MODEL
MOE_GROUPS = 4
EXPERTS_PER_GROUP = 8
N_EXPERTS = MOE_GROUPS * EXPERTS_PER_GROUP
TOP_K_IN_GROUP = 2
D_EXPERT = 128
ALPHA = (2.0 * DEPTH) ** 0.25
BETA = (8.0 * DEPTH) ** -0.25
LN_EPS = 1e-5

kernel_name = "hybrid_rwkv7_pool_conformer_retention_hmoe"


def layer_norm(x, g, b, eps=LN_EPS):
    xf = x.astype(jnp.float32)
    mu = jnp.mean(xf, axis=-1, keepdims=True)
    var = jnp.mean(jnp.square(xf - mu), axis=-1, keepdims=True)
    return ((xf - mu) * lax.rsqrt(var + eps) * g + b).astype(x.dtype)


def head_norm(y, eps):
    yf = y.astype(jnp.float32)
    mu = jnp.mean(yf, axis=-1, keepdims=True)
    var = jnp.mean(jnp.square(yf - mu), axis=-1, keepdims=True)
    out = (yf - mu) * lax.rsqrt(var + eps)
    return out.reshape(y.shape[:-2] + (-1,))


def rwkv7_scan(r, w, k, v, kk, a):
    B, S, H, d = r.shape

    def step(state, inp):
        r_t, w_t, k_t, v_t, kk_t, a_t = inp
        sa = jnp.einsum('bhvk,bhk->bhv', state, -kk_t)
        state = (state * w_t[:, :, None, :]
                 + sa[..., None] * (kk_t * a_t)[:, :, None, :]
                 + v_t[..., None] * k_t[:, :, None, :])
        y_t = jnp.einsum('bhvk,bhk->bhv', state, r_t)
        return state, y_t

    xs = (jnp.moveaxis(r, 1, 0), jnp.moveaxis(w, 1, 0), jnp.moveaxis(k, 1, 0),
          jnp.moveaxis(v, 1, 0), jnp.moveaxis(kk, 1, 0), jnp.moveaxis(a, 1, 0))
    init = jnp.zeros((B, H, d, d), jnp.float32)
    _, y = lax.scan(step, init, xs)
    return jnp.moveaxis(y, 0, 1)


def multiscale_pool(u):
    B, S, _ = u.shape
    uf = u.astype(jnp.float32).reshape(B, S, POOL_GROUPS, POOL_GROUP_DIM)
    c = jnp.cumsum(uf, axis=1)
    t = jnp.arange(S)
    outs = []
    for gi, win in enumerate(POOL_WINDOWS):
        cg = c[:, :, gi]
        c_prev = jnp.pad(cg, ((0, 0), (win, 0), (0, 0)))[:, :S]
        count = jnp.minimum(t + 1, win).astype(jnp.float32)[None, :, None]
        outs.append((cg - c_prev) / count - uf[:, :, gi])
    return jnp.stack(outs, axis=2)


def rwkv_pool_mixer(x, w_in, mu, w0, w2, a0, a2, g2, k_k, k_a, r_k,
                    lnx_g, lnx_b, pool_w, pool_scale, w_out):
    B, S, _ = x.shape
    p = x @ w_in
    pr, u = p[..., :RWKV_PROJ], p[..., RWKV_PROJ:]
    prev = jnp.pad(pr, ((0, 0), (1, 0), (0, 0)))[:, :-1]
    z = pr + mu * (prev - pr)
    c1 = RWKV_WIDTH
    r, k, v, wlo, alo, glo = jnp.split(
        z, [c1, 2 * c1, 3 * c1, 3 * c1 + W_LORA, 3 * c1 + W_LORA + A_LORA], axis=-1)
    wl = -jax.nn.softplus(-(w0 + jnp.tanh(wlo) @ w2)) - 0.5
    decay = jnp.exp(-jnp.exp(wl.astype(jnp.float32)))
    a = jax.nn.sigmoid(a0 + alo @ a2)
    g = jax.nn.sigmoid(glo) @ g2

    def hs(t):
        return t.astype(jnp.float32).reshape(B, S, RWKV_HEADS, RWKV_HEAD_DIM)

    kk = hs(k * k_k)
    kk = kk * lax.rsqrt(jnp.maximum(jnp.sum(jnp.square(kk), -1, keepdims=True), 1e-24))
    k = k * (1.0 + (a - 1.0) * k_a)
    rh, kh, vh, ah, dh = hs(r), hs(k), hs(v), hs(a), hs(decay)
    y = rwkv7_scan(rh, dh, kh, vh, kk, ah)
    y = head_norm(y, RWKV_NORM_EPS) * lnx_g + lnx_b
    bonus = (jnp.sum(rh * kh * r_k, axis=-1, keepdims=True) * vh).reshape(B, S, RWKV_WIDTH)
    y_rwkv = ((y + bonus) * g).astype(x.dtype)

    pooled = multiscale_pool(u).astype(x.dtype)
    y_pool = jnp.einsum('bsgc,gcd->bsgd', pooled, pool_w).reshape(B, S, POOL_WIDTH) * pool_scale
    return jnp.concatenate([y_rwkv, y_pool.astype(x.dtype)], axis=-1) @ w_out


def rotary(t, pos):
    half = t.shape[-1] // 2
    inv = ROPE_BASE ** (-jnp.arange(half, dtype=jnp.float32) / half)
    ang = pos[:, None] * inv[None, :]
    cos = jnp.cos(ang)[None, :, None, :]
    sin = jnp.sin(ang)[None, :, None, :]
    t1, t2 = t[..., :half], t[..., half:]
    return jnp.concatenate([t1 * cos - t2 * sin, t1 * sin + t2 * cos], axis=-1)


def retention(q, k, v):
    B, S, _ = q.shape
    H, d, C = RET_HEADS, RET_HEAD_DIM, RET_CHUNK
    N = S // C
    pos = jnp.arange(S, dtype=jnp.float32)
    qh = rotary(q.astype(jnp.float32).reshape(B, S, H, d), pos)
    kh = rotary(k.astype(jnp.float32).reshape(B, S, H, d), pos) * (d ** -0.5)
    vh = v.astype(jnp.float32).reshape(B, S, H, d)

    def to_chunks(t):
        return t.reshape(B, N, C, H, d).transpose(0, 3, 1, 2, 4)

    qc, kc, vc = to_chunks(qh), to_chunks(kh), to_chunks(vh)
    log_gamma = jnp.log1p(-jnp.power(2.0, -5.0 - jnp.arange(H, dtype=jnp.float32)))
    idx = jnp.arange(C, dtype=jnp.float32)
    diff = idx[:, None] - idx[None, :]
    decay_mask = jnp.where(diff >= 0,
                           jnp.exp(jnp.maximum(diff, 0.0)[None] * log_gamma[:, None, None]),
                           0.0)
    scores = jnp.einsum('bhnid,bhnjd->bhnij', qc, kc) * decay_mask[None, :, None]
    intra = jnp.einsum('bhnij,bhnjd->bhnid', scores, vc)
    xi = jnp.exp((idx + 1.0)[None, :] * log_gamma[:, None])
    zeta = jnp.exp((C - 1.0 - idx)[None, :] * log_gamma[:, None])
    gamma_chunk = jnp.exp(C * log_gamma)
    kv = jnp.einsum('bhncd,bhnce->bhnde', kc * zeta[None, :, None, :, None], vc)

    def step(R, kv_n):
        return gamma_chunk[None, :, None, None] * R + kv_n, R

    _, R_prev = lax.scan(step, jnp.zeros((B, H, d, d), jnp.float32), jnp.moveaxis(kv, 2, 0))
    R_prev = jnp.moveaxis(R_prev, 0, 2)
    cross = jnp.einsum('bhncd,bhnde->bhnce', qc * xi[None, :, None, :, None], R_prev)
    return (intra + cross).transpose(0, 2, 3, 1, 4).reshape(B, S, H, d)


def conv_retention_mixer(x, w_in, conv_w, conv_b, cln_g, cln_b, gn_g, gn_b, w_out):
    p = x @ w_in
    cc, rw = CONV_CHANNELS, RET_WIDTH
    ca, cb, q, k, v, gr = jnp.split(
        p, [cc, 2 * cc, 2 * cc + rw, 2 * cc + 2 * rw, 2 * cc + 3 * rw], axis=-1)
    u = ca * jax.nn.sigmoid(cb)
    u = lax.conv_general_dilated(
        u, conv_w[:, None, :], window_strides=(1,), padding=[(CONV_WIDTH - 1, 0)],
        dimension_numbers=('NWC', 'WIO', 'NWC'), feature_group_count=CONV_CHANNELS) + conv_b
    y_conv = jax.nn.silu(layer_norm(u, cln_g, cln_b))
    ret = head_norm(retention(q, k, v), LN_EPS) * gn_g + gn_b
    y_ret = (jax.nn.silu(gr) * ret).astype(x.dtype)
    return jnp.concatenate([y_conv.astype(x.dtype), y_ret], axis=-1) @ w_out


def hier_moe(x, rg_w, rg_b, re_w, re_b, w1, w3, w2):
    B, S, D = x.shape
    xf = x.reshape(B * S, D)
    g_logits = (xf @ rg_w + rg_b).astype(jnp.float32)
    g_prob = jax.nn.softmax(g_logits, axis=-1)
    g_idx = jnp.argmax(g_logits, axis=-1)
    g_w = jnp.max(g_prob, axis=-1, keepdims=True)
    e_logits = (xf @ re_w + re_b).astype(jnp.float32).reshape(-1, MOE_GROUPS, EXPERTS_PER_GROUP)
    e_sel = jnp.einsum('nge,ng->ne', e_logits, jax.nn.one_hot(g_idx, MOE_GROUPS, dtype=jnp.float32))
    e_prob = jax.nn.softmax(e_sel, axis=-1)
    top_p, top_i = lax.top_k(e_prob, TOP_K_IN_GROUP)
    top_p = top_p / jnp.sum(top_p, axis=-1, keepdims=True)
    global_idx = g_idx[:, None] * EXPERTS_PER_GROUP + top_i
    gates = jnp.sum(jax.nn.one_hot(global_idx, N_EXPERTS, dtype=jnp.float32)
                    * (g_w * top_p)[..., None], axis=1)

    def expert(acc, params):
        w1e, w3e, w2e, ge = params
        h = jax.nn.silu(xf @ w1e) * (xf @ w3e)
        return acc + ge[:, None] * (h @ w2e), None

    out, _ = lax.scan(expert, jnp.zeros_like(xf), (w1, w3, w2, gates.T.astype(x.dtype)))
    return out.reshape(B, S, D)


def setup_inputs(seed: int = 0) -> dict:
    key = jax.random.key(seed)
    ks = iter(jax.random.split(key, 48))
    ne, no = (DEPTH + 1) // 2, DEPTH // 2
    D = D_MODEL

    def nrm(shape, scale):
        return jax.random.normal(next(ks), shape, jnp.float32) * scale

    def gain(shape):
        return 1.0 + nrm(shape, 0.02)

    lin = jnp.linspace(0.0, 1.0, RWKV_WIDTH, dtype=jnp.float32)
    return {
        "x": nrm((BATCH, SEQ, D), 1.0),
        "ev_w_in": nrm((ne, D, EVEN_PROJ), D ** -0.5),
        "ev_mu": jax.random.uniform(next(ks), (ne, RWKV_PROJ), jnp.float32, 0.2, 0.8),
        "ev_w0": -6.5 + 5.0 * lin ** 0.9 + nrm((ne, RWKV_WIDTH), 0.1),
        "ev_w2": nrm((ne, W_LORA, RWKV_WIDTH), 0.1),
        "ev_a0": nrm((ne, RWKV_WIDTH), 0.1),
        "ev_a2": nrm((ne, A_LORA, RWKV_WIDTH), A_LORA ** -0.5),
        "ev_g2": nrm((ne, G_LORA, RWKV_WIDTH), G_LORA ** -0.5),
        "ev_k_k": 0.85 + nrm((ne, RWKV_WIDTH), 0.02),
        "ev_k_a": gain((ne, RWKV_WIDTH)),
        "ev_r_k": nrm((ne, RWKV_HEADS, RWKV_HEAD_DIM), 0.1),
        "ev_lnx_g": gain((ne, RWKV_WIDTH)),
        "ev_lnx_b": nrm((ne, RWKV_WIDTH), 0.02),
        "ev_pool_w": nrm((ne, POOL_GROUPS, POOL_GROUP_DIM, POOL_GROUP_DIM), POOL_GROUP_DIM ** -0.5),
        "ev_pool_scale": gain((ne, POOL_WIDTH)),
        "ev_w_out": nrm((ne, MIX_WIDTH, D), MIX_WIDTH ** -0.5 * BETA),
        "od_w_in": nrm((no, D, ODD_PROJ), D ** -0.5),
        "od_conv_w": nrm((no, CONV_WIDTH, CONV_CHANNELS), CONV_WIDTH ** -0.5),
        "od_conv_b": nrm((no, CONV_CHANNELS), 0.02),
        "od_cln_g": gain((no, CONV_CHANNELS)),
        "od_cln_b": nrm((no, CONV_CHANNELS), 0.02),
        "od_gn_g": gain((no, RET_WIDTH)),
        "od_gn_b": nrm((no, RET_WIDTH), 0.02),
        "od_w_out": nrm((no, MIX_WIDTH, D), MIX_WIDTH ** -0.5 * BETA),
        "ln_mix_g": gain((DEPTH, D)),
        "ln_mix_b": nrm((DEPTH, D), 0.02),
        "rg_w": nrm((DEPTH, D, MOE_GROUPS), D ** -0.5),
        "rg_b": nrm((DEPTH, MOE_GROUPS), 0.01),
        "re_w": nrm((DEPTH, D, N_EXPERTS), D ** -0.5),
        "re_b": nrm((DEPTH, N_EXPERTS), 0.01),
        "e_w1": nrm((DEPTH, N_EXPERTS, D, D_EXPERT), D ** -0.5),
        "e_w3": nrm((DEPTH, N_EXPERTS, D, D_EXPERT), D ** -0.5),
        "e_w2": nrm((DEPTH, N_EXPERTS, D_EXPERT, D), D_EXPERT ** -0.5 * BETA),
        "ln_ffn_g": gain((DEPTH, D)),
        "ln_ffn_b": nrm((DEPTH, D), 0.02),
    }


def reference(x, ev_w_in, ev_mu, ev_w0, ev_w2, ev_a0, ev_a2, ev_g2, ev_k_k, ev_k_a,
              ev_r_k, ev_lnx_g, ev_lnx_b, ev_pool_w, ev_pool_scale, ev_w_out,
              od_w_in, od_conv_w, od_conv_b, od_cln_g, od_cln_b, od_gn_g, od_gn_b,
              od_w_out, ln_mix_g, ln_mix_b, rg_w, rg_b, re_w, re_b, e_w1, e_w3, e_w2,
              ln_ffn_g, ln_ffn_b):
    for i in range(DEPTH):
        j = i // 2
        if i % 2 == 0:
            h = rwkv_pool_mixer(x, ev_w_in[j], ev_mu[j], ev_w0[j], ev_w2[j], ev_a0[j],
                                ev_a2[j], ev_g2[j], ev_k_k[j], ev_k_a[j], ev_r_k[j],
                                ev_lnx_g[j], ev_lnx_b[j], ev_pool_w[j], ev_pool_scale[j],
                                ev_w_out[j])
        else:
            h = conv_retention_mixer(x, od_w_in[j], od_conv_w[j], od_conv_b[j], od_cln_g[j],
                                     od_cln_b[j], od_gn_g[j], od_gn_b[j], od_w_out[j])
        x = layer_norm(ALPHA * x + h, ln_mix_g[i], ln_mix_b[i])
        f = hier_moe(x, rg_w[i], rg_b[i], re_w[i], re_b[i], e_w1[i], e_w3[i], e_w2[i])
        x = layer_norm(ALPHA * x + f, ln_ffn_g[i], ln_ffn_b[i])
    return x
```

```python
import functools
import math

import numpy as np
import jax
import jax.numpy as jnp
from jax import lax
from jax.experimental import pallas as pl
from jax.experimental.pallas import tpu as pltpu

F32 = jnp.float32
BF16 = jnp.bfloat16

D_MODEL = 1024
HALF = D_MODEL // 2
HEAD_DIM = 64
N_HEADS = HALF // HEAD_DIM
PAIR = 2 * HEAD_DIM
N_PAIRS = N_HEADS // 2
LORA_W = 64
LORA_A = 64
LORA_G = 128
RWKV_PROJ = 3 * HALF + LORA_W + LORA_A + LORA_G
EVEN_PROJ = RWKV_PROJ + HALF
ODD_PROJ = 6 * HALF
RWKV_NORM_EPS = 64e-5
LN_EPS = 1e-5
POOL_WINDOWS = (2, 4, 8, 16)
POOL_HALO = 16
CONV_WIDTH = 31
CONV_HALO = 32
ROPE_BASE = 10000.0
MOE_GROUPS = 4
EXPERTS_PER_GROUP = 8
N_EXPERTS = MOE_GROUPS * EXPERTS_PER_GROUP
D_EXPERT = 128
DEPTH = 2
ALPHA = (2.0 * DEPTH) ** 0.25
LANES = 128
NEG_BIG = -1e30

ROW_TILE = 512
SCAN_CHUNK = 64
RET_CHUNK = 256
MOE_ROW_TILE = 512
MOE_EXPERT_BLOCK = 8
VMEM_LIMIT = 56 * 1024 * 1024


def _bf(v):
    return v.astype(BF16)


def _dot(a, b):
    return jnp.dot(a, b, preferred_element_type=F32)


def _dot_nt(a, b):
    return lax.dot_general(a, b, (((1,), (1,)), ((), ())), preferred_element_type=F32)


def _dot_tn(a, b):
    return lax.dot_general(a, b, (((0,), (0,)), ((), ())), preferred_element_type=F32)


def _dotb(a, b):
    return _dot(_bf(a), _bf(b))


def _split_dot(a, b_bf16):
    hi = _bf(a)
    lo = _bf(a - hi.astype(F32))
    return _dot(hi, b_bf16) + _dot(lo, b_bf16)


def _sigmoid(v):
    return 1.0 / (1.0 + jnp.exp(-v))


def _layer_norm(v, g, b, eps):
    mu = jnp.mean(v, axis=-1, keepdims=True)
    d = v - mu
    var = jnp.mean(d * d, axis=-1, keepdims=True)
    return d * lax.rsqrt(var + eps) * g + b


def _head_norm(v, ones_bd, eps):
    mu = _split_dot(v, ones_bd) * (1.0 / HEAD_DIM)
    d = v - mu
    var = _split_dot(d * d, ones_bd) * (1.0 / HEAD_DIM)
    return d * lax.rsqrt(var + eps)


def _full(shape):
    nd = len(shape)
    return pl.BlockSpec(shape, lambda *_: (0,) * nd)


def _params(sem):
    return pltpu.CompilerParams(dimension_semantics=sem, vmem_limit_bytes=VMEM_LIMIT)


def _ev_front_kernel(x_ref, win_ref, mu_ref, w0_ref, w2p_ref, a0_ref, a2p_ref, g2_ref, kkw_ref,
                     ka_ref, rk_ref, ones_ref, poolw_ref, pscale_ref,
                     r_o, lw_o, k_o, v_o, kk_o, a_o, g_o, bonus_o, ypool_o,
                     prow_s, ucarry_s):
    t = pl.program_id(1)
    tm = x_ref.shape[0]

    @pl.when(t == 0)
    def _():
        prow_s[...] = jnp.zeros_like(prow_s)
        ucarry_s[...] = jnp.zeros_like(ucarry_s)

    p = _dot(_bf(x_ref[...]), win_ref[...])
    pr = p[:, :RWKV_PROJ]
    u = p[:, RWKV_PROJ:]

    row = lax.broadcasted_iota(jnp.int32, (tm, RWKV_PROJ), 0)
    prev = jnp.where(row == 0, prow_s[0:1, :], pltpu.roll(pr, 1, 0))
    prow_s[0:1, :] = pr[tm - 1:tm, :]
    z = pr + mu_ref[...] * (prev - pr)

    r = z[:, 0:HALF]
    k = z[:, HALF:2 * HALF]
    v = z[:, 2 * HALF:3 * HALF]
    zl = z[:, 3 * HALF:3 * HALF + LORA_W + LORA_A]
    zg = z[:, 3 * HALF + LORA_W + LORA_A:RWKV_PROJ]

    yw = w0_ref[...] + _dotb(jnp.tanh(zl), w2p_ref[...])
    lw = -math.exp(-0.5) * _sigmoid(yw)
    a = _sigmoid(a0_ref[...] + _dotb(zl, a2p_ref[...]))
    g = _dotb(_sigmoid(zg), g2_ref[...])

    ones_bd = ones_ref[...]
    kk = k * kkw_ref[...]
    kk = kk * lax.rsqrt(jnp.maximum(_split_dot(kk * kk, ones_bd), 1e-24))
    kmod = k * (1.0 + (a - 1.0) * ka_ref[...])
    bonus = _split_dot(r * kmod * rk_ref[...], ones_bd) * v

    r_o[...] = r
    lw_o[...] = lw
    k_o[...] = kmod
    v_o[...] = v
    kk_o[...] = kk
    a_o[...] = a
    g_o[...] = g
    bonus_o[...] = bonus

    ext = jnp.concatenate([ucarry_s[...], u], axis=0)
    ucarry_s[...] = u[tm - POOL_HALO:, :]
    pos = t * tm + lax.broadcasted_iota(jnp.int32, (tm, LANES), 0)
    for gi, win in enumerate(POOL_WINDOWS):
        s = ext[:, gi * LANES:(gi + 1) * LANES]
        for step in range(gi + 1):
            s = s + pltpu.roll(s, 2 ** step, 0)
        count = jnp.minimum(pos + 1, win).astype(F32)
        u_g = u[:, gi * LANES:(gi + 1) * LANES]
        pooled = s[POOL_HALO:, :] / count - u_g
        ypool_o[:, gi * LANES:(gi + 1) * LANES] = (
            _dotb(pooled, poolw_ref[gi]) * pscale_ref[:, gi * LANES:(gi + 1) * LANES])


def _ev_front(x2, bsz, seq, win, mu, w0, w2p, a0, a2p, g2, kkw, ka, rk, ones_bd, poolw, pscale):
    tm = min(ROW_TILE, seq)
    nt = seq // tm
    n = bsz * seq
    row_spec = lambda w: pl.BlockSpec((tm, w), lambda b, t: (b * nt + t, 0))
    consts = (win, mu, w0, w2p, a0, a2p, g2, kkw, ka, rk, ones_bd, poolw, pscale)
    return pl.pallas_call(
        _ev_front_kernel,
        grid=(bsz, nt),
        in_specs=[row_spec(D_MODEL)] + [_full(c.shape) for c in consts],
        out_specs=[row_spec(HALF)] * 9,
        out_shape=[jax.ShapeDtypeStruct((n, HALF), F32)] * 9,
        scratch_shapes=[pltpu.VMEM((8, RWKV_PROJ), F32), pltpu.VMEM((POOL_HALO, HALF), F32)],
        compiler_params=_params(("arbitrary", "arbitrary")),
        name="ev_front",
    )(x2, *consts)


def _scan_masks(c):
    n = 2 * c
    i = np.arange(n)[:, None]
    j = np.arange(n)[None, :]
    same = (i // c) == (j // c)
    masks = [same & (i > j), same & (i >= j), (i == j), (i // 8 == j // 8) & (i > j)]
    b = 8
    while b < c:
        masks.append((i // (2 * b) == j // (2 * b)) & (i // b != j // b) & (i > j))
        b *= 2
    return np.stack(masks).astype(np.float32)


def _rwkv_scan_kernel(r_ref, lw_ref, k_ref, v_ref, kk_ref, a_ref, tri_ref, masks_ref, y_o, h_s):
    c = r_ref.shape[0]

    @pl.when(pl.program_id(1) == 0)
    def _():
        h_s[...] = jnp.zeros_like(h_s)

    lw = lw_ref[...]
    cl = _split_dot_lhs(tri_ref[...], lw)
    cl_last = cl[c - 1:c, :]
    e_in = jnp.exp(cl)
    e_ex = jnp.exp(cl - lw)
    e_neg = jnp.exp(-cl)
    e_rem = jnp.exp(cl_last - cl)
    p_last = jnp.exp(cl_last)

    kk = kk_ref[...]
    kv = k_ref[...]
    beta = kk * a_ref[...]
    abar = -kk * e_ex
    rbar = r_ref[...] * e_in
    btil = beta * e_neg
    ktil = kv * e_neg
    bhat = beta * e_rem
    khat = kv * e_rem
    vv = v_ref[...]

    m_strict = masks_ref[0]
    m_incl = masks_ref[1]
    eye = masks_ref[2]
    m_blk = masks_ref[3]
    n_merge = masks_ref.shape[0] - 4

    lane = lax.broadcasted_iota(jnp.int32, (c, PAIR), 1)
    even = lane < HEAD_DIM

    def stack_bd(m):
        return jnp.concatenate([jnp.where(even, m, 0.0), jnp.where(even, 0.0, m)], axis=0)

    def stack_2(m):
        return jnp.concatenate([m, m], axis=0)

    for pr in range(N_PAIRS):
        sl = slice(pr * PAIR, (pr + 1) * PAIR)
        abar_bd = stack_bd(abar[:, sl])
        rbar_bd = stack_bd(rbar[:, sl])
        v_bd = stack_bd(vv[:, sl])
        bhat_bd = stack_bd(bhat[:, sl])
        khat_bd = stack_bd(khat[:, sl])
        lhs = _bf(jnp.concatenate([abar_bd, rbar_bd], axis=0))
        rhs = _bf(jnp.concatenate([stack_2(btil[:, sl]), stack_2(ktil[:, sl])], axis=0))
        sc = _dot_nt(lhs, rhs)
        n = 2 * c
        a_ab = sc[:n, :n] * m_strict
        a_ak = sc[:n, n:] * m_strict
        a_rb = sc[n:, :n] * m_incl
        a_rk = sc[n:, n:] * m_incl

        a_d = a_ab * m_blk
        tinv = eye + a_d
        pw = _dotb(a_d, a_d)
        tinv = _dotb(tinv, eye + pw)
        pw = _dotb(pw, pw)
        tinv = _dotb(tinv, eye + pw)
        for lvl in range(n_merge):
            a_off = a_ab * masks_ref[4 + lvl]
            tinv = tinv + _dotb(tinv, _dotb(a_off, tinv))

        v_bd16 = _bf(v_bd)
        akv = _dot(_bf(a_ak), v_bd16)
        wu = _dotb(tinv, jnp.concatenate([abar_bd, akv], axis=1))
        wu16 = _bf(wu)
        arb_wu = _dot(_bf(a_rb), wu16)
        qhat = rbar_bd + arb_wu[:, :PAIR]
        y0 = arb_wu[:, PAIR:] + _dot(_bf(a_rk), v_bd16)
        mg = _dot_tn(_bf(bhat_bd), wu16)
        m_mat = mg[:, :PAIR]
        g_mat = mg[:, PAIR:] + _dot_tn(_bf(khat_bd), v_bd16)

        h0 = h_s[pr]
        h16 = _bf(h0)
        y_bd = _dot(_bf(qhat), h16) + y0
        y_o[:, sl] = y_bd[:c, :] + y_bd[c:, :]
        p_col = jnp.sum(eye * p_last[:, sl], axis=1, keepdims=True)
        h_s[pr] = p_col * h0 + _dot(_bf(m_mat), h16) + g_mat


def _split_dot_lhs(a_bf16, b):
    hi = _bf(b)
    lo = _bf(b - hi.astype(F32))
    return _dot(a_bf16, hi) + _dot(a_bf16, lo)


def _rwkv_scan(r, lw, k, v, kk, a, bsz, seq):
    c = min(SCAN_CHUNK, seq)
    nc = seq // c
    n = bsz * seq
    tri = jnp.asarray(np.tril(np.ones((c, c), np.float32)), BF16)
    masks = jnp.asarray(_scan_masks(c))
    row_spec = pl.BlockSpec((c, HALF), lambda b, t: (b * nc + t, 0))
    return pl.pallas_call(
        _rwkv_scan_kernel,
        grid=(bsz, nc),
        in_specs=[row_spec] * 6 + [_full(tri.shape), _full(masks.shape)],
        out_specs=row_spec,
        out_shape=jax.ShapeDtypeStruct((n, HALF), F32),
        scratch_shapes=[pltpu.VMEM((N_PAIRS, PAIR, PAIR), F32)],
        compiler_params=_params(("arbitrary", "arbitrary")),
        name="rwkv_scan",
    )(r, lw, k, v, kk, a, tri, masks)


def _ev_back_kernel(y_ref, g_ref, bonus_ref, ypool_ref, x_ref, ones_ref, lng_ref, lnb_ref,
                    wout_ref, mg_ref, mb_ref, o_ref):
    hn = _head_norm(y_ref[...], ones_ref[...], RWKV_NORM_EPS)
    y_rwkv = (hn * lng_ref[...] + lnb_ref[...] + bonus_ref[...]) * g_ref[...]
    h = _dot(_bf(y_rwkv), wout_ref[0:HALF, :]) + _dot(_bf(ypool_ref[...]), wout_ref[HALF:, :])
    o_ref[...] = _layer_norm(ALPHA * x_ref[...] + h, mg_ref[...], mb_ref[...], LN_EPS)


def _ev_back(y, g, bonus, ypool, x2, ones_bd, lng, lnb, wout, mg, mb):
    n = x2.shape[0]
    tm = min(ROW_TILE, n)
    half_spec = pl.BlockSpec((tm, HALF), lambda i: (i, 0))
    full_spec = pl.BlockSpec((tm, D_MODEL), lambda i: (i, 0))
    consts = (ones_bd, lng, lnb, wout, mg, mb)
    return pl.pallas_call(
        _ev_back_kernel,
        grid=(n // tm,),
        in_specs=[half_spec] * 4 + [full_spec] + [_full(c.shape) for c in consts],
        out_specs=full_spec,
        out_shape=jax.ShapeDtypeStruct((n, D_MODEL), F32),
        compiler_params=_params(("parallel",)),
        name="ev_back",
    )(y, g, bonus, ypool, x2, *consts)


def _moe_kernel(x_ref, rwh_ref, rwl_ref, rb_ref, w1_ref, w3_ref, w2_ref, ex_ref, fg_ref, fb_ref,
                o_ref, xb_s, gates_s, acc_s):
    j = pl.program_id(1)

    @pl.when(j == 0)
    def _():
        x = x_ref[...]
        xh = _bf(x)
        xl = _bf(x - xh.astype(F32))
        xb_s[...] = xh
        logits = (_dot(xh, rwh_ref[...]) + _dot(xl, rwh_ref[...]) + _dot(xh, rwl_ref[...])
                  + rb_ref[...])
        lane = lax.broadcasted_iota(jnp.int32, logits.shape, 1)
        lanef = lane.astype(F32)
        is_g = (lane >= N_EXPERTS) & (lane < N_EXPERTS + MOE_GROUPS)
        gl = jnp.where(is_g, logits, NEG_BIG)
        gmax = jnp.max(gl, axis=1, keepdims=True)
        gidx = jnp.min(jnp.where(gl == gmax, lanef, 1e9), axis=1, keepdims=True) - float(N_EXPERTS)
        gden = jnp.sum(jnp.where(is_g, jnp.exp(gl - gmax), 0.0), axis=1, keepdims=True)
        g_w = 1.0 / gden
        grp = jnp.floor(lanef * (1.0 / EXPERTS_PER_GROUP))
        sel = (lane < N_EXPERTS) & (grp == gidx)
        el = jnp.where(sel, logits, NEG_BIG)
        m1 = jnp.max(el, axis=1, keepdims=True)
        i1 = jnp.min(jnp.where(el == m1, lanef, 1e9), axis=1, keepdims=True)
        el2 = jnp.where(lanef == i1, NEG_BIG, el)
        m2 = jnp.max(el2, axis=1, keepdims=True)
        i2 = jnp.min(jnp.where(el2 == m2, lanef, 1e9), axis=1, keepdims=True)
        e21 = jnp.exp(m2 - m1)
        w_top = g_w / (1.0 + e21)
        w_sec = g_w * e21 / (1.0 + e21)
        gates_s[...] = jnp.where(lanef == i1, w_top, jnp.where(lanef == i2, w_sec, 0.0))
        acc_s[...] = jnp.zeros_like(acc_s)

    xb = xb_s[...]
    h1 = _dot(xb, w1_ref[...])
    h3 = _dot(xb, w3_ref[...])
    gexp = _split_dot(gates_s[...], ex_ref[...])
    act = h1 * _sigmoid(h1) * h3 * gexp
    acc_s[...] += _dot(_bf(act), w2_ref[...])

    @pl.when(j == pl.num_programs(1) - 1)
    def _():
        o_ref[...] = _layer_norm(ALPHA * x_ref[...] + acc_s[...], fg_ref[...], fb_ref[...], LN_EPS)


def _moe(x2, rwh, rwl, rb, w1, w3, w2, expand, fg, fb):
    n = x2.shape[0]
    tm = min(MOE_ROW_TILE, n)
    eb = MOE_EXPERT_BLOCK * D_EXPERT
    nj = N_EXPERTS // MOE_EXPERT_BLOCK
    x_spec = pl.BlockSpec((tm, D_MODEL), lambda i, j: (i, 0))
    return pl.pallas_call(
        _moe_kernel,
        grid=(n // tm, nj),
        in_specs=[x_spec, _full(rwh.shape), _full(rwl.shape), _full(rb.shape),
                  pl.BlockSpec((D_MODEL, eb), lambda i, j: (0, j)),
                  pl.BlockSpec((D_MODEL, eb), lambda i, j: (0, j)),
                  pl.BlockSpec((eb, D_MODEL), lambda i, j: (j, 0)),
                  pl.BlockSpec((LANES, eb), lambda i, j: (0, j)),
                  _full(fg.shape), _full(fb.shape)],
        out_specs=x_spec,
        out_shape=jax.ShapeDtypeStruct((n, D_MODEL), F32),
        scratch_shapes=[pltpu.VMEM((tm, D_MODEL), BF16), pltpu.VMEM((tm, LANES), F32),
                        pltpu.VMEM((tm, D_MODEL), F32)],
        compiler_params=_params(("parallel", "arbitrary")),
        name="moe",
    )(x2, rwh, rwl, rb, w1, w3, w2, expand, fg, fb)


def _rope_table_kernel(cos_o, sin_o):
    tm = cos_o.shape[0]
    half = HEAD_DIM // 2
    pos = (pl.program_id(0) * tm + lax.broadcasted_iota(jnp.int32, (tm, LANES), 0)).astype(F32)
    lane = lax.broadcasted_iota(jnp.int32, (tm, LANES), 1)
    idx = (lane & (half - 1)).astype(F32)
    inv = jnp.exp(idx * (-math.log(ROPE_BASE) / half))
    ang = pos * inv
    first = (lane & half) == 0
    c = jnp.cos(ang)
    s = jnp.sin(ang)
    s = jnp.where(first, -s, s)
    for q in range(HALF // LANES):
        cos_o[:, q * LANES:(q + 1) * LANES] = c
        sin_o[:, q * LANES:(q + 1) * LANES] = s


def _rope_table(seq):
    tm = min(ROW_TILE, seq)
    spec = pl.BlockSpec((tm, HALF), lambda i: (i, 0))
    return pl.pallas_call(
        _rope_table_kernel,
        grid=(seq // tm,),
        in_specs=[],
        out_specs=[spec, spec],
        out_shape=[jax.ShapeDtypeStruct((seq, HALF), F32)] * 2,
        compiler_params=_params(("parallel",)),
        name="rope_table",
    )()


def _od_front_kernel(x_ref, win_ref, cw_ref, cb_ref, clg_ref, clb_ref, cos_ref, sin_ref,
                     yconv_o, q_o, k_o, v_o, sg_o, ubuf_s):
    t = pl.program_id(1)
    tm = x_ref.shape[0]

    @pl.when(t == 0)
    def _():
        ubuf_s[0:CONV_HALO, :] = jnp.zeros((CONV_HALO, HALF), F32)

    p = _dot(_bf(x_ref[...]), win_ref[...])
    ca = p[:, 0:HALF]
    cb = p[:, HALF:2 * HALF]
    q = p[:, 2 * HALF:3 * HALF]
    k = p[:, 3 * HALF:4 * HALF]
    v = p[:, 4 * HALF:5 * HALF]
    gr = p[:, 5 * HALF:6 * HALF]

    ubuf_s[CONV_HALO:, :] = ca * _sigmoid(cb)
    acc = jnp.zeros((tm, HALF), F32) + cb_ref[...]
    for j in range(CONV_WIDTH):
        off = CONV_HALO - (CONV_WIDTH - 1) + j
        acc = acc + cw_ref[j:j + 1, :] * ubuf_s[off:off + tm, :]
    ubuf_s[0:CONV_HALO, :] = ubuf_s[tm:tm + CONV_HALO, :]
    ln = _layer_norm(acc, clg_ref[...], clb_ref[...], LN_EPS)
    yconv_o[...] = ln * _sigmoid(ln)

    lane = lax.broadcasted_iota(jnp.int32, (tm, HALF), 1)
    first = (lane & (HEAD_DIM // 2)) == 0
    cos = cos_ref[...]
    sin = sin_ref[...]

    def rot(m):
        partner = jnp.where(first, pltpu.roll(m, HALF - HEAD_DIM // 2, 1), pltpu.roll(m, HEAD_DIM // 2, 1))
        return m * cos + partner * sin

    q_o[...] = _bf(rot(q))
    k_o[...] = _bf(rot(k) * (HEAD_DIM ** -0.5))
    v_o[...] = _bf(v)
    sg_o[...] = gr * _sigmoid(gr)


def _od_front(x2, bsz, seq, win, cw, cb, clg, clb, cos_t, sin_t):
    tm = min(ROW_TILE, seq)
    nt = seq // tm
    n = bsz * seq
    row_spec = lambda w: pl.BlockSpec((tm, w), lambda b, t: (b * nt + t, 0))
    tab_spec = pl.BlockSpec((tm, HALF), lambda b, t: (t, 0))
    consts = (win, cw, cb, clg, clb)
    sds = lambda dt: jax.ShapeDtypeStruct((n, HALF), dt)
    return pl.pallas_call(
        _od_front_kernel,
        grid=(bsz, nt),
        in_specs=[row_spec(D_MODEL)] + [_full(c.shape) for c in consts] + [tab_spec, tab_spec],
        out_specs=[row_spec(HALF)] * 5,
        out_shape=[sds(F32), sds(BF16), sds(BF16), sds(BF16), sds(F32)],
        scratch_shapes=[pltpu.VMEM((tm + CONV_HALO, HALF), F32)],
        compiler_params=_params(("arbitrary", "arbitrary")),
        name="od_front",
    )(x2, *consts, cos_t, sin_t)


def _ret_consts(c):
    h = np.arange(N_HEADS, dtype=np.float64)
    log_gamma = np.log1p(-np.power(2.0, -5.0 - h))
    idx = np.arange(c, dtype=np.float64)
    diff = idx[:, None] - idx[None, :]
    dmask = np.where(diff >= 0, np.exp(np.maximum(diff, 0.0)[None] * log_gamma[:, None, None]), 0.0)
    xi = np.exp((idx + 1.0)[:, None] * log_gamma[None, :])
    zeta = np.exp((c - 1.0 - idx)[:, None] * log_gamma[None, :])
    xi = np.repeat(xi, HEAD_DIM, axis=1)
    zeta = np.repeat(zeta, HEAD_DIM, axis=1)
    gamma_c = np.exp(c * log_gamma)
    lane_head = np.arange(PAIR) // HEAD_DIM
    gdiag = np.zeros((N_PAIRS, PAIR, PAIR))
    for pr in range(N_PAIRS):
        same = lane_head[:, None] == lane_head[None, :]
        gdiag[pr] = np.where(same, gamma_c[2 * pr + lane_head][:, None], 0.0)
    bd = (lane_head[:, None] == lane_head[None, :]).astype(np.float32)
    return (dmask.astype(np.float32), xi.astype(np.float32), zeta.astype(np.float32),
            gdiag.astype(np.float32), bd)


def _retention_kernel(q_ref, k_ref, v_ref, sg_ref, dmask_ref, xi_ref, zeta_ref, gdiag_ref, bd_ref,
                      ones_ref, gng_ref, gnb_ref, o_ref, r_s):
    c = q_ref.shape[0]

    @pl.when(pl.program_id(1) == 0)
    def _():
        r_s[...] = jnp.zeros_like(r_s)

    lane = lax.broadcasted_iota(jnp.int32, (c, PAIR), 1)
    even = lane < HEAD_DIM
    bd = bd_ref[...]
    for pr in range(N_PAIRS):
        sl = slice(pr * PAIR, (pr + 1) * PAIR)
        qp = q_ref[:, sl]
        kp = k_ref[:, sl]
        vp = v_ref[:, sl]
        qf = qp.astype(F32)
        zero = jnp.zeros_like(qp)
        s_even = _dot_nt(jnp.where(even, qp, zero), kp) * dmask_ref[2 * pr]
        s_odd = _dot_nt(jnp.where(even, zero, qp), kp) * dmask_ref[2 * pr + 1]
        intra = jnp.where(even, _dot(_bf(s_even), vp), _dot(_bf(s_odd), vp))
        r0 = r_s[pr]
        cross = _dot(_bf(qf * xi_ref[:, sl]), _bf(r0))
        kz = _bf(kp.astype(F32) * zeta_ref[:, sl])
        r_s[pr] = gdiag_ref[pr] * r0 + bd * _dot_tn(kz, vp)
        ret = _head_norm(intra + cross, ones_ref[...], LN_EPS)
        o_ref[:, sl] = sg_ref[:, sl] * (ret * gng_ref[:, sl] + gnb_ref[:, sl])


def _retention(q, k, v, sg, bsz, seq, gng, gnb):
    c = min(RET_CHUNK, seq)
    nc = seq // c
    n = bsz * seq
    dmask, xi, zeta, gdiag, bd = (jnp.asarray(m) for m in _ret_consts(c))
    ones_pair = jnp.asarray(np.kron(np.eye(2), np.ones((HEAD_DIM, HEAD_DIM))), BF16)
    row_spec = pl.BlockSpec((c, HALF), lambda b, t: (b * nc + t, 0))
    consts = (dmask, xi, zeta, gdiag, bd, ones_pair, gng, gnb)
    return pl.pallas_call(
        _retention_kernel,
        grid=(bsz, nc),
        in_specs=[row_spec] * 4 + [_full(m.shape) for m in consts],
        out_specs=row_spec,
        out_shape=jax.ShapeDtypeStruct((n, HALF), F32),
        scratch_shapes=[pltpu.VMEM((N_PAIRS, PAIR, PAIR), F32)],
        compiler_params=_params(("arbitrary", "arbitrary")),
        name="retention",
    )(q, k, v, sg, *consts)


def _od_back_kernel(ya_ref, yb_ref, x_ref, wout_ref, mg_ref, mb_ref, o_ref):
    h = _dot(_bf(ya_ref[...]), wout_ref[0:HALF, :]) + _dot(_bf(yb_ref[...]), wout_ref[HALF:, :])
    o_ref[...] = _layer_norm(ALPHA * x_ref[...] + h, mg_ref[...], mb_ref[...], LN_EPS)


def _od_back(ya, yb, x2, wout, mg, mb):
    n = x2.shape[0]
    tm = min(ROW_TILE, n)
    half_spec = pl.BlockSpec((tm, HALF), lambda i: (i, 0))
    full_spec = pl.BlockSpec((tm, D_MODEL), lambda i: (i, 0))
    consts = (wout, mg, mb)
    return pl.pallas_call(
        _od_back_kernel,
        grid=(n // tm,),
        in_specs=[half_spec, half_spec, full_spec] + [_full(c.shape) for c in consts],
        out_specs=full_spec,
        out_shape=jax.ShapeDtypeStruct((n, D_MODEL), F32),
        compiler_params=_params(("parallel",)),
        name="od_back",
    )(ya, yb, x2, *consts)


def _row(v):
    return v.reshape(1, -1).astype(F32)


def _moe_weights(rg_w, rg_b, re_w, re_b, e_w1, e_w3, e_w2):
    pad = LANES - N_EXPERTS - MOE_GROUPS
    rw = jnp.concatenate([re_w, rg_w, jnp.zeros((D_MODEL, pad), F32)], axis=1)
    rwh = _bf(rw)
    rwl = _bf(rw - rwh.astype(F32))
    rb = jnp.concatenate([re_b, rg_b, jnp.zeros((pad,), F32)]).reshape(1, LANES)
    w1 = _bf(e_w1.transpose(1, 0, 2).reshape(D_MODEL, N_EXPERTS * D_EXPERT))
    w3 = _bf(e_w3.transpose(1, 0, 2).reshape(D_MODEL, N_EXPERTS * D_EXPERT))
    w2 = _bf(e_w2.reshape(N_EXPERTS * D_EXPERT, D_MODEL))
    return rwh, rwl, rb, w1, w3, w2


def kernel(x, ev_w_in, ev_mu, ev_w0, ev_w2, ev_a0, ev_a2, ev_g2, ev_k_k, ev_k_a, ev_r_k, ev_lnx_g, ev_lnx_b, ev_pool_w, ev_pool_scale, ev_w_out, od_w_in, od_conv_w, od_conv_b, od_cln_g, od_cln_b, od_gn_g, od_gn_b, od_w_out, ln_mix_g, ln_mix_b, rg_w, rg_b, re_w, re_b, e_w1, e_w3, e_w2, ln_ffn_g, ln_ffn_b):
    bsz, seq, _ = x.shape
    x2 = x.reshape(bsz * seq, D_MODEL)
    ones_bd = jnp.asarray(np.kron(np.eye(N_HEADS), np.ones((HEAD_DIM, HEAD_DIM))), BF16)
    expand = jnp.asarray(
        np.pad(np.kron(np.eye(N_EXPERTS), np.ones((1, D_EXPERT))), ((0, LANES - N_EXPERTS), (0, 0))), BF16)
    zeros_lora = jnp.zeros((LORA_W, HALF), F32)

    w2p = jnp.concatenate([ev_w2[0], zeros_lora], axis=0)
    a2p = jnp.concatenate([zeros_lora, ev_a2[0]], axis=0)
    r, lw, k, v, kk, a, g, bonus, ypool = _ev_front(
        x2, bsz, seq, _bf(ev_w_in[0]), _row(ev_mu[0]), _row(ev_w0[0]), _bf(w2p), _row(ev_a0[0]), _bf(a2p),
        _bf(ev_g2[0]), _row(ev_k_k[0]), _row(ev_k_a[0]), _row(ev_r_k[0]), ones_bd, _bf(ev_pool_w[0]),
        _row(ev_pool_scale[0]))
    y = _rwkv_scan(r, lw, k, v, kk, a, bsz, seq)
    x2 = _ev_back(y, g, bonus, ypool, x2, ones_bd, _row(ev_lnx_g[0]), _row(ev_lnx_b[0]), _bf(ev_w_out[0]),
                  _row(ln_mix_g[0]), _row(ln_mix_b[0]))
    x2 = _moe(x2, *_moe_weights(rg_w[0], rg_b[0], re_w[0], re_b[0], e_w1[0], e_w3[0], e_w2[0]), expand,
              _row(ln_ffn_g[0]), _row(ln_ffn_b[0]))

    cos_t, sin_t = _rope_table(seq)
    yconv, q, kr, vr, sg = _od_front(x2, bsz, seq, _bf(od_w_in[0]), od_conv_w[0], _row(od_conv_b[0]),
                                     _row(od_cln_g[0]), _row(od_cln_b[0]), cos_t, sin_t)
    yret = _retention(q, kr, vr, sg, bsz, seq, _row(od_gn_g[0]), _row(od_gn_b[0]))
    x2 = _od_back(yconv, yret, x2, _bf(od_w_out[0]), _row(ln_mix_g[1]), _row(ln_mix_b[1]))
    x2 = _moe(x2, *_moe_weights(rg_w[1], rg_b[1], re_w[1], re_b[1], e_w1[1], e_w3[1], e_w2[1]), expand,
              _row(ln_ffn_g[1]), _row(ln_ffn_b[1]))
    return x2.reshape(bsz, seq, D_MODEL)
```

```python
import functools
import math

import numpy as np
import jax
import jax.numpy as jnp
from jax import lax
from jax.experimental import pallas as pl
from jax.experimental.pallas import tpu as pltpu

F32 = jnp.float32
BF16 = jnp.bfloat16

D_MODEL = 1024
HALF = D_MODEL // 2
HEAD_DIM = 64
N_HEADS = HALF // HEAD_DIM
PAIR = 2 * HEAD_DIM
N_PAIRS = N_HEADS // 2
LORA_W = 64
LORA_A = 64
LORA_G = 128
RWKV_PROJ = 3 * HALF + LORA_W + LORA_A + LORA_G
EVEN_PROJ = RWKV_PROJ + HALF
ODD_PROJ = 6 * HALF
RWKV_NORM_EPS = 64e-5
LN_EPS = 1e-5
POOL_WINDOWS = (2, 4, 8, 16)
POOL_HALO = 16
CONV_WIDTH = 31
CONV_HALO = 32
ROPE_BASE = 10000.0
MOE_GROUPS = 4
EXPERTS_PER_GROUP = 8
N_EXPERTS = MOE_GROUPS * EXPERTS_PER_GROUP
D_EXPERT = 128
DEPTH = 2
ALPHA = (2.0 * DEPTH) ** 0.25
LANES = 128
SUBLANES = 8
NEG_BIG = -1e30

ROW_TILE = 512
SCAN_CHUNK = 64
SCAN_ROWS = 256
RET_CHUNK = 256
MOE_ROW_TILE = 512
MOE_EXPERT_BLOCK = 8
VMEM_LIMIT = 56 * 1024 * 1024


def _bf(v):
    return v.astype(BF16)


def _dot(a, b):
    return jnp.dot(a, b, preferred_element_type=F32)


def _dot_nt(a, b):
    return lax.dot_general(a, b, (((1,), (1,)), ((), ())), preferred_element_type=F32)


def _dot_tn(a, b):
    return lax.dot_general(a, b, (((0,), (0,)), ((), ())), preferred_element_type=F32)


def _dotb(a, b):
    return _dot(_bf(a), _bf(b))


def _split_dot(a, b_bf16):
    hi = _bf(a)
    lo = _bf(a - hi.astype(F32))
    return _dot(hi, b_bf16) + _dot(lo, b_bf16)


def _sigmoid(v):
    return 1.0 / (1.0 + jnp.exp(-v))


def _layer_norm(v, g, b, eps):
    mu = jnp.mean(v, axis=-1, keepdims=True)
    d = v - mu
    var = jnp.mean(d * d, axis=-1, keepdims=True)
    return d * lax.rsqrt(var + eps) * g + b


def _head_norm(v, ones_bd, eps):
    mu = _split_dot(v, ones_bd) * (1.0 / HEAD_DIM)
    d = v - mu
    var = _split_dot(d * d, ones_bd) * (1.0 / HEAD_DIM)
    return d * lax.rsqrt(var + eps)


def _full(shape):
    nd = len(shape)
    return pl.BlockSpec(shape, lambda *_: (0,) * nd)


def _params(sem):
    return pltpu.CompilerParams(dimension_semantics=sem, vmem_limit_bytes=VMEM_LIMIT)


def _ev_front_kernel(x_ref, win_ref, mu_ref, w0_ref, w2p_ref, a0_ref, a2p_ref, g2_ref, kkw_ref,
                     ka_ref, rk_ref, ones_ref, poolw_ref, pscale_ref,
                     r_o, lw_o, k_o, v_o, kk_o, a_o, g_o, bonus_o, ypool_o,
                     prow_s, ucarry_s):
    t = pl.program_id(1)
    tm = x_ref.shape[0]

    @pl.when(t == 0)
    def _():
        prow_s[...] = jnp.zeros_like(prow_s)
        ucarry_s[...] = jnp.zeros_like(ucarry_s)

    p = _dot(_bf(x_ref[...]), win_ref[...])
    pr = p[:, :RWKV_PROJ]
    u = p[:, RWKV_PROJ:]

    row = lax.broadcasted_iota(jnp.int32, (tm, RWKV_PROJ), 0)
    prev = jnp.where(row == 0, prow_s[0:1, :], pltpu.roll(pr, 1, 0))
    prow_s[0:1, :] = pr[tm - 1:tm, :]
    z = pr + mu_ref[...] * (prev - pr)

    r = z[:, 0:HALF]
    k = z[:, HALF:2 * HALF]
    v = z[:, 2 * HALF:3 * HALF]
    zl = z[:, 3 * HALF:3 * HALF + LORA_W + LORA_A]
    zg = z[:, 3 * HALF + LORA_W + LORA_A:RWKV_PROJ]

    yw = w0_ref[...] + _dotb(jnp.tanh(zl), w2p_ref[...])
    lw = -math.exp(-0.5) * _sigmoid(yw)
    a = _sigmoid(a0_ref[...] + _dotb(zl, a2p_ref[...]))
    g = _dotb(_sigmoid(zg), g2_ref[...])

    ones_bd = ones_ref[...]
    kk = k * kkw_ref[...]
    kk = kk * lax.rsqrt(jnp.maximum(_split_dot(kk * kk, ones_bd), 1e-24))
    kmod = k * (1.0 + (a - 1.0) * ka_ref[...])
    bonus = _split_dot(r * kmod * rk_ref[...], ones_bd) * v

    r_o[...] = r
    lw_o[...] = lw
    k_o[...] = kmod
    v_o[...] = v
    kk_o[...] = kk
    a_o[...] = a
    g_o[...] = g
    bonus_o[...] = bonus

    ext = jnp.concatenate([ucarry_s[...], u], axis=0)
    ucarry_s[...] = u[tm - POOL_HALO:, :]
    pos = t * tm + lax.broadcasted_iota(jnp.int32, (tm, LANES), 0)
    for gi, win in enumerate(POOL_WINDOWS):
        s = ext[:, gi * LANES:(gi + 1) * LANES]
        for step in range(gi + 1):
            s = s + pltpu.roll(s, 2 ** step, 0)
        count = jnp.minimum(pos + 1, win).astype(F32)
        u_g = u[:, gi * LANES:(gi + 1) * LANES]
        pooled = s[POOL_HALO:, :] / count - u_g
        ypool_o[:, gi * LANES:(gi + 1) * LANES] = (
            _dotb(pooled, poolw_ref[gi]) * pscale_ref[:, gi * LANES:(gi + 1) * LANES])


def _ev_front(x2, bsz, seq, win, mu, w0, w2p, a0, a2p, g2, kkw, ka, rk, ones_bd, poolw, pscale):
    tm = min(ROW_TILE, seq)
    nt = seq // tm
    n = bsz * seq
    row_spec = lambda w: pl.BlockSpec((tm, w), lambda b, t: (b * nt + t, 0))
    consts = (win, mu, w0, w2p, a0, a2p, g2, kkw, ka, rk, ones_bd, poolw, pscale)
    return pl.pallas_call(
        _ev_front_kernel,
        grid=(bsz, nt),
        in_specs=[row_spec(D_MODEL)] + [_full(c.shape) for c in consts],
        out_specs=[row_spec(HALF)] * 9,
        out_shape=[jax.ShapeDtypeStruct((n, HALF), F32)] * 9,
        scratch_shapes=[pltpu.VMEM((8, RWKV_PROJ), F32), pltpu.VMEM((POOL_HALO, HALF), F32)],
        compiler_params=_params(("arbitrary", "arbitrary")),
        name="ev_front",
    )(x2, *consts)


def _scan_masks(c):
    n = 2 * c
    i = np.arange(n)[:, None]
    j = np.arange(n)[None, :]
    same = (i // c) == (j // c)
    masks = [same & (i > j), same & (i >= j), (i == j), (i // 8 == j // 8) & (i > j)]
    b = 8
    while b < c:
        masks.append((i // (2 * b) == j // (2 * b)) & (i // b != j // b) & (i > j))
        b *= 2
    return np.stack(masks).astype(np.float32)


def _rwkv_scan_kernel(r_ref, lw_ref, k_ref, v_ref, kk_ref, a_ref, tri_ref, masks_ref, y_o, h_s):
    c = SCAN_CHUNK
    rows = r_ref.shape[0]
    n_chunks = rows // c
    n = 2 * c

    @pl.when(pl.program_id(1) == 0)
    def _():
        h_s[...] = jnp.zeros_like(h_s)

    lw = lw_ref[...]
    cl = _split_dot_lhs(tri_ref[...], lw)
    cl_last_rows = [cl[(g + 1) * c - 1:(g + 1) * c, :] for g in range(n_chunks)]
    cl_last = jnp.concatenate([jnp.broadcast_to(m, (c, HALF)) for m in cl_last_rows], axis=0)
    e_in = jnp.exp(cl)
    e_ex = jnp.exp(cl - lw)
    e_neg = jnp.exp(-cl)
    e_rem = jnp.exp(cl_last - cl)

    kk = kk_ref[...]
    kv = k_ref[...]
    beta = kk * a_ref[...]
    abar = -kk * e_ex
    rbar = r_ref[...] * e_in
    btil = beta * e_neg
    ktil = kv * e_neg
    bhat = beta * e_rem
    khat = kv * e_rem
    vv = v_ref[...]

    m_strict = masks_ref[0]
    m_incl = masks_ref[1]
    eye = masks_ref[2]
    m_blk = masks_ref[3]
    n_merge = masks_ref.shape[0] - 4

    lane = lax.broadcasted_iota(jnp.int32, (c, PAIR), 1)
    even = lane < HEAD_DIM

    def stack_bd(m):
        return jnp.concatenate([jnp.where(even, m, 0.0), jnp.where(even, 0.0, m)], axis=0)

    def stack_2(m):
        return jnp.concatenate([m, m], axis=0)

    items = [(g, pr) for g in range(n_chunks) for pr in range(N_PAIRS)]

    def cut(m, it):
        g, pr = it
        return m[g * c:(g + 1) * c, pr * PAIR:(pr + 1) * PAIR]

    abar_bd = [stack_bd(cut(abar, it)) for it in items]
    rbar_bd = [stack_bd(cut(rbar, it)) for it in items]
    v_bd16 = [_bf(stack_bd(cut(vv, it))) for it in items]
    bhat_bd16 = [_bf(stack_bd(cut(bhat, it))) for it in items]
    khat_bd16 = [_bf(stack_bd(cut(khat, it))) for it in items]
    sc = [_dot_nt(_bf(jnp.concatenate([abar_bd[i], rbar_bd[i]], axis=0)),
                  _bf(jnp.concatenate([stack_2(cut(btil, it)), stack_2(cut(ktil, it))], axis=0)))
          for i, it in enumerate(items)]
    a_ab = [m[:n, :n] * m_strict for m in sc]
    a_ak16 = [_bf(m[:n, n:] * m_strict) for m in sc]
    a_rb16 = [_bf(m[n:, :n] * m_incl) for m in sc]
    a_rk16 = [_bf(m[n:, n:] * m_incl) for m in sc]

    a_d = [m * m_blk for m in a_ab]
    pw = [_dotb(m, m) for m in a_d]
    tinv = [_dotb(eye + a_d[i], eye + pw[i]) for i in range(len(items))]
    pw = [_dotb(m, m) for m in pw]
    tinv = [_dotb(tinv[i], eye + pw[i]) for i in range(len(items))]
    for lvl in range(n_merge):
        m_off = masks_ref[4 + lvl]
        tinv16 = [_bf(m) for m in tinv]
        at = [_dot(_bf(a_ab[i] * m_off), tinv16[i]) for i in range(len(items))]
        tinv = [tinv[i] + _dot(tinv16[i], _bf(at[i])) for i in range(len(items))]

    akv = [_dot(a_ak16[i], v_bd16[i]) for i in range(len(items))]
    wu16 = [_bf(_dotb(tinv[i], jnp.concatenate([abar_bd[i], akv[i]], axis=1)))
            for i in range(len(items))]
    arb_wu = [_dot(a_rb16[i], wu16[i]) for i in range(len(items))]
    qhat16 = [_bf(rbar_bd[i] + arb_wu[i][:, :PAIR]) for i in range(len(items))]
    y0 = [arb_wu[i][:, PAIR:] + _dot(a_rk16[i], v_bd16[i]) for i in range(len(items))]
    mg = [_dot_tn(bhat_bd16[i], wu16[i]) for i in range(len(items))]
    m_mat16 = [_bf(m[:, :PAIR]) for m in mg]
    g_mat = [mg[i][:, PAIR:] + _dot_tn(khat_bd16[i], v_bd16[i]) for i in range(len(items))]

    h = [h_s[pr] for pr in range(N_PAIRS)]
    for i, (g, pr) in enumerate(items):
        h16 = _bf(h[pr])
        y_bd = _dot(qhat16[i], h16) + y0[i]
        y_o[g * c:(g + 1) * c, pr * PAIR:(pr + 1) * PAIR] = y_bd[:c, :] + y_bd[c:, :]
        p_last = jnp.exp(cl_last_rows[g][:, pr * PAIR:(pr + 1) * PAIR])
        p_col = jnp.sum(eye * p_last, axis=1, keepdims=True)
        h[pr] = p_col * h[pr] + _dot(m_mat16[i], h16) + g_mat[i]
    for pr in range(N_PAIRS):
        h_s[pr] = h[pr]


def _split_dot_lhs(a_bf16, b):
    hi = _bf(b)
    lo = _bf(b - hi.astype(F32))
    return _dot(a_bf16, hi) + _dot(a_bf16, lo)


def _rwkv_scan(r, lw, k, v, kk, a, bsz, seq):
    c = SCAN_CHUNK
    rows = min(SCAN_ROWS, seq)
    nb = seq // rows
    n = bsz * seq
    tri = jnp.asarray(np.kron(np.eye(rows // c), np.tril(np.ones((c, c)))), BF16)
    masks = jnp.asarray(_scan_masks(c))
    row_spec = pl.BlockSpec((rows, HALF), lambda b, t: (b * nb + t, 0))
    return pl.pallas_call(
        _rwkv_scan_kernel,
        grid=(bsz, nb),
        in_specs=[row_spec] * 6 + [_full(tri.shape), _full(masks.shape)],
        out_specs=row_spec,
        out_shape=jax.ShapeDtypeStruct((n, HALF), F32),
        scratch_shapes=[pltpu.VMEM((N_PAIRS, PAIR, PAIR), F32)],
        compiler_params=_params(("arbitrary", "arbitrary")),
        name="rwkv_scan",
    )(r, lw, k, v, kk, a, tri, masks)


def _ev_back_kernel(y_ref, g_ref, bonus_ref, ypool_ref, x_ref, ones_ref, lng_ref, lnb_ref,
                    wout_ref, mg_ref, mb_ref, o_ref):
    hn = _head_norm(y_ref[...], ones_ref[...], RWKV_NORM_EPS)
    y_rwkv = (hn * lng_ref[...] + lnb_ref[...] + bonus_ref[...]) * g_ref[...]
    h = _dot(_bf(y_rwkv), wout_ref[0:HALF, :]) + _dot(_bf(ypool_ref[...]), wout_ref[HALF:, :])
    o_ref[...] = _layer_norm(ALPHA * x_ref[...] + h, mg_ref[...], mb_ref[...], LN_EPS)


def _ev_back(y, g, bonus, ypool, x2, ones_bd, lng, lnb, wout, mg, mb):
    n = x2.shape[0]
    tm = min(ROW_TILE, n)
    half_spec = pl.BlockSpec((tm, HALF), lambda i: (i, 0))
    full_spec = pl.BlockSpec((tm, D_MODEL), lambda i: (i, 0))
    consts = (ones_bd, lng, lnb, wout, mg, mb)
    return pl.pallas_call(
        _ev_back_kernel,
        grid=(n // tm,),
        in_specs=[half_spec] * 4 + [full_spec] + [_full(c.shape) for c in consts],
        out_specs=full_spec,
        out_shape=jax.ShapeDtypeStruct((n, D_MODEL), F32),
        compiler_params=_params(("parallel",)),
        name="ev_back",
    )(y, g, bonus, ypool, x2, *consts)


def _moe_kernel(x_ref, rwh_ref, rwl_ref, rb_ref, w1_ref, w3_ref, w2_ref, ex_ref, fg_ref, fb_ref,
                o_ref, xb_s, gates_s, acc_s):
    j = pl.program_id(1)

    @pl.when(j == 0)
    def _():
        x = x_ref[...]
        xh = _bf(x)
        xl = _bf(x - xh.astype(F32))
        xb_s[...] = xh
        logits = (_dot(xh, rwh_ref[...]) + _dot(xl, rwh_ref[...]) + _dot(xh, rwl_ref[...])
                  + rb_ref[...])
        lane = lax.broadcasted_iota(jnp.int32, logits.shape, 1)
        lanef = lane.astype(F32)
        is_g = (lane >= N_EXPERTS) & (lane < N_EXPERTS + MOE_GROUPS)
        gl = jnp.where(is_g, logits, NEG_BIG)
        gmax = jnp.max(gl, axis=1, keepdims=True)
        gidx = jnp.min(jnp.where(gl == gmax, lanef, 1e9), axis=1, keepdims=True) - float(N_EXPERTS)
        gden = jnp.sum(jnp.where(is_g, jnp.exp(gl - gmax), 0.0), axis=1, keepdims=True)
        g_w = 1.0 / gden
        grp = jnp.floor(lanef * (1.0 / EXPERTS_PER_GROUP))
        sel = (lane < N_EXPERTS) & (grp == gidx)
        el = jnp.where(sel, logits, NEG_BIG)
        m1 = jnp.max(el, axis=1, keepdims=True)
        i1 = jnp.min(jnp.where(el == m1, lanef, 1e9), axis=1, keepdims=True)
        el2 = jnp.where(lanef == i1, NEG_BIG, el)
        m2 = jnp.max(el2, axis=1, keepdims=True)
        i2 = jnp.min(jnp.where(el2 == m2, lanef, 1e9), axis=1, keepdims=True)
        e21 = jnp.exp(m2 - m1)
        w_top = g_w / (1.0 + e21)
        w_sec = g_w * e21 / (1.0 + e21)
        gates_s[...] = jnp.where(lanef == i1, w_top, jnp.where(lanef == i2, w_sec, 0.0))
        acc_s[...] = jnp.zeros_like(acc_s)

    xb = xb_s[...]
    h1 = _dot(xb, w1_ref[...])
    h3 = _dot(xb, w3_ref[...])
    gexp = _split_dot(gates_s[...], ex_ref[...])
    act = h1 * _sigmoid(h1) * h3 * gexp
    acc_s[...] += _dot(_bf(act), w2_ref[...])

    @pl.when(j == pl.num_programs(1) - 1)
    def _():
        o_ref[...] = _layer_norm(ALPHA * x_ref[...] + acc_s[...], fg_ref[...], fb_ref[...], LN_EPS)


def _moe(x2, rwh, rwl, rb, w1, w3, w2, expand, fg, fb):
    n = x2.shape[0]
    tm = min(MOE_ROW_TILE, n)
    eb = MOE_EXPERT_BLOCK * D_EXPERT
    nj = N_EXPERTS // MOE_EXPERT_BLOCK
    x_spec = pl.BlockSpec((tm, D_MODEL), lambda i, j: (i, 0))
    return pl.pallas_call(
        _moe_kernel,
        grid=(n // tm, nj),
        in_specs=[x_spec, _full(rwh.shape), _full(rwl.shape), _full(rb.shape),
                  pl.BlockSpec((D_MODEL, eb), lambda i, j: (0, j)),
                  pl.BlockSpec((D_MODEL, eb), lambda i, j: (0, j)),
                  pl.BlockSpec((eb, D_MODEL), lambda i, j: (j, 0)),
                  pl.BlockSpec((LANES, eb), lambda i, j: (0, j)),
                  _full(fg.shape), _full(fb.shape)],
        out_specs=x_spec,
        out_shape=jax.ShapeDtypeStruct((n, D_MODEL), F32),
        scratch_shapes=[pltpu.VMEM((tm, D_MODEL), BF16), pltpu.VMEM((tm, LANES), F32),
                        pltpu.VMEM((tm, D_MODEL), F32)],
        compiler_params=_params(("parallel", "arbitrary")),
        name="moe",
    )(x2, rwh, rwl, rb, w1, w3, w2, expand, fg, fb)


def _rope_table_kernel(cos_o, sin_o):
    tm = cos_o.shape[0]
    half = HEAD_DIM // 2
    pos = (pl.program_id(0) * tm + lax.broadcasted_iota(jnp.int32, (tm, LANES), 0)).astype(F32)
    lane = lax.broadcasted_iota(jnp.int32, (tm, LANES), 1)
    idx = (lane & (half - 1)).astype(F32)
    inv = jnp.exp(idx * (-math.log(ROPE_BASE) / half))
    ang = pos * inv
    first = (lane & half) == 0
    c = jnp.cos(ang)
    s = jnp.sin(ang)
    s = jnp.where(first, -s, s)
    for q in range(HALF // LANES):
        cos_o[:, q * LANES:(q + 1) * LANES] = c
        sin_o[:, q * LANES:(q + 1) * LANES] = s


def _rope_table(seq):
    tm = min(ROW_TILE, seq)
    spec = pl.BlockSpec((tm, HALF), lambda i: (i, 0))
    return pl.pallas_call(
        _rope_table_kernel,
        grid=(seq // tm,),
        in_specs=[],
        out_specs=[spec, spec],
        out_shape=[jax.ShapeDtypeStruct((seq, HALF), F32)] * 2,
        compiler_params=_params(("parallel",)),
        name="rope_table",
    )()


def _od_front_kernel(x_ref, win_ref, cw_ref, cb_ref, clg_ref, clb_ref, cos_ref, sin_ref,
                     yconv_o, q_o, k_o, v_o, sg_o, ubuf_s):
    t = pl.program_id(1)
    tm = x_ref.shape[0]

    @pl.when(t == 0)
    def _():
        ubuf_s[0:CONV_HALO, :] = jnp.zeros((CONV_HALO, HALF), F32)

    p = _dot(_bf(x_ref[...]), win_ref[...])
    ca = p[:, 0:HALF]
    cb = p[:, HALF:2 * HALF]
    q = p[:, 2 * HALF:3 * HALF]
    k = p[:, 3 * HALF:4 * HALF]
    v = p[:, 4 * HALF:5 * HALF]
    gr = p[:, 5 * HALF:6 * HALF]

    ubuf_s[CONV_HALO:, :] = ca * _sigmoid(cb)
    acc = jnp.zeros((tm, HALF), F32) + cb_ref[...]
    for rho in range(SUBLANES):
        ext = 0 if rho == 0 else SUBLANES
        part = None
        for j in range(CONV_WIDTH):
            off = CONV_HALO - (CONV_WIDTH - 1) + j
            if off % SUBLANES != rho:
                continue
            base = off - rho
            term = cw_ref[j:j + 1, :] * ubuf_s[base:base + tm + ext, :]
            part = term if part is None else part + term
        if rho == 0:
            acc = acc + part
        else:
            acc = acc + pltpu.roll(part, tm + ext - rho, 0)[0:tm, :]
    ubuf_s[0:CONV_HALO, :] = ubuf_s[tm:tm + CONV_HALO, :]
    ln = _layer_norm(acc, clg_ref[...], clb_ref[...], LN_EPS)
    yconv_o[...] = ln * _sigmoid(ln)

    lane = lax.broadcasted_iota(jnp.int32, (tm, HALF), 1)
    first = (lane & (HEAD_DIM // 2)) == 0
    cos = cos_ref[...]
    sin = sin_ref[...]

    def rot(m):
        partner = jnp.where(first, pltpu.roll(m, HALF - HEAD_DIM // 2, 1), pltpu.roll(m, HEAD_DIM // 2, 1))
        return m * cos + partner * sin

    q_o[...] = _bf(rot(q))
    k_o[...] = _bf(rot(k) * (HEAD_DIM ** -0.5))
    v_o[...] = _bf(v)
    sg_o[...] = gr * _sigmoid(gr)


def _od_front(x2, bsz, seq, win, cw, cb, clg, clb, cos_t, sin_t):
    tm = min(ROW_TILE, seq)
    nt = seq // tm
    n = bsz * seq
    row_spec = lambda w: pl.BlockSpec((tm, w), lambda b, t: (b * nt + t, 0))
    tab_spec = pl.BlockSpec((tm, HALF), lambda b, t: (t, 0))
    consts = (win, cw, cb, clg, clb)
    sds = lambda dt: jax.ShapeDtypeStruct((n, HALF), dt)
    return pl.pallas_call(
        _od_front_kernel,
        grid=(bsz, nt),
        in_specs=[row_spec(D_MODEL)] + [_full(c.shape) for c in consts] + [tab_spec, tab_spec],
        out_specs=[row_spec(HALF)] * 5,
        out_shape=[sds(F32), sds(BF16), sds(BF16), sds(BF16), sds(F32)],
        scratch_shapes=[pltpu.VMEM((tm + CONV_HALO, HALF), F32)],
        compiler_params=_params(("arbitrary", "arbitrary")),
        name="od_front",
    )(x2, *consts, cos_t, sin_t)


def _ret_consts(c):
    h = np.arange(N_HEADS, dtype=np.float64)
    log_gamma = np.log1p(-np.power(2.0, -5.0 - h))
    idx = np.arange(c, dtype=np.float64)
    diff = idx[:, None] - idx[None, :]
    dmask = np.where(diff >= 0, np.exp(np.maximum(diff, 0.0)[None] * log_gamma[:, None, None]), 0.0)
    xi = np.exp((idx + 1.0)[:, None] * log_gamma[None, :])
    zeta = np.exp((c - 1.0 - idx)[:, None] * log_gamma[None, :])
    xi = np.repeat(xi, HEAD_DIM, axis=1)
    zeta = np.repeat(zeta, HEAD_DIM, axis=1)
    gamma_c = np.exp(c * log_gamma)
    lane_head = np.arange(PAIR) // HEAD_DIM
    gdiag = np.zeros((N_PAIRS, PAIR, PAIR))
    for pr in range(N_PAIRS):
        same = lane_head[:, None] == lane_head[None, :]
        gdiag[pr] = np.where(same, gamma_c[2 * pr + lane_head][:, None], 0.0)
    bd = (lane_head[:, None] == lane_head[None, :]).astype(np.float32)
    return (dmask.astype(np.float32), xi.astype(np.float32), zeta.astype(np.float32),
            gdiag.astype(np.float32), bd)


def _retention_kernel(q_ref, k_ref, v_ref, sg_ref, dmask_ref, xi_ref, zeta_ref, gdiag_ref, bd_ref,
                      ones_ref, gng_ref, gnb_ref, o_ref, r_s):
    c = q_ref.shape[0]

    @pl.when(pl.program_id(1) == 0)
    def _():
        r_s[...] = jnp.zeros_like(r_s)

    lane = lax.broadcasted_iota(jnp.int32, (c, PAIR), 1)
    even = lane < HEAD_DIM
    bd = bd_ref[...]
    for pr in range(N_PAIRS):
        sl = slice(pr * PAIR, (pr + 1) * PAIR)
        qp = q_ref[:, sl]
        kp = k_ref[:, sl]
        vp = v_ref[:, sl]
        qf = qp.astype(F32)
        zero = jnp.zeros_like(qp)
        s_even = _dot_nt(jnp.where(even, qp, zero), kp) * dmask_ref[2 * pr]
        s_odd = _dot_nt(jnp.where(even, zero, qp), kp) * dmask_ref[2 * pr + 1]
        intra = jnp.where(even, _dot(_bf(s_even), vp), _dot(_bf(s_odd), vp))
        r0 = r_s[pr]
        cross = _dot(_bf(qf * xi_ref[:, sl]), _bf(r0))
        kz = _bf(kp.astype(F32) * zeta_ref[:, sl])
        r_s[pr] = gdiag_ref[pr] * r0 + bd * _dot_tn(kz, vp)
        ret = _head_norm(intra + cross, ones_ref[...], LN_EPS)
        o_ref[:, sl] = sg_ref[:, sl] * (ret * gng_ref[:, sl] + gnb_ref[:, sl])


def _retention(q, k, v, sg, bsz, seq, gng, gnb):
    c = min(RET_CHUNK, seq)
    nc = seq // c
    n = bsz * seq
    dmask, xi, zeta, gdiag, bd = (jnp.asarray(m) for m in _ret_consts(c))
    ones_pair = jnp.asarray(np.kron(np.eye(2), np.ones((HEAD_DIM, HEAD_DIM))), BF16)
    row_spec = pl.BlockSpec((c, HALF), lambda b, t: (b * nc + t, 0))
    consts = (dmask, xi, zeta, gdiag, bd, ones_pair, gng, gnb)
    return pl.pallas_call(
        _retention_kernel,
        grid=(bsz, nc),
        in_specs=[row_spec] * 4 + [_full(m.shape) for m in consts],
        out_specs=row_spec,
        out_shape=jax.ShapeDtypeStruct((n, HALF), F32),
        scratch_shapes=[pltpu.VMEM((N_PAIRS, PAIR, PAIR), F32)],
        compiler_params=_params(("arbitrary", "arbitrary")),
        name="retention",
    )(q, k, v, sg, *consts)


def _od_back_kernel(ya_ref, yb_ref, x_ref, wout_ref, mg_ref, mb_ref, o_ref):
    h = _dot(_bf(ya_ref[...]), wout_ref[0:HALF, :]) + _dot(_bf(yb_ref[...]), wout_ref[HALF:, :])
    o_ref[...] = _layer_norm(ALPHA * x_ref[...] + h, mg_ref[...], mb_ref[...], LN_EPS)


def _od_back(ya, yb, x2, wout, mg, mb):
    n = x2.shape[0]
    tm = min(ROW_TILE, n)
    half_spec = pl.BlockSpec((tm, HALF), lambda i: (i, 0))
    full_spec = pl.BlockSpec((tm, D_MODEL), lambda i: (i, 0))
    consts = (wout, mg, mb)
    return pl.pallas_call(
        _od_back_kernel,
        grid=(n // tm,),
        in_specs=[half_spec, half_spec, full_spec] + [_full(c.shape) for c in consts],
        out_specs=full_spec,
        out_shape=jax.ShapeDtypeStruct((n, D_MODEL), F32),
        compiler_params=_params(("parallel",)),
        name="od_back",
    )(ya, yb, x2, *consts)


def _row(v):
    return v.reshape(1, -1).astype(F32)


def _moe_weights(rg_w, rg_b, re_w, re_b, e_w1, e_w3, e_w2):
    pad = LANES - N_EXPERTS - MOE_GROUPS
    rw = jnp.concatenate([re_w, rg_w, jnp.zeros((D_MODEL, pad), F32)], axis=1)
    rwh = _bf(rw)
    rwl = _bf(rw - rwh.astype(F32))
    rb = jnp.concatenate([re_b, rg_b, jnp.zeros((pad,), F32)]).reshape(1, LANES)
    w1 = _bf(e_w1.transpose(1, 0, 2).reshape(D_MODEL, N_EXPERTS * D_EXPERT))
    w3 = _bf(e_w3.transpose(1, 0, 2).reshape(D_MODEL, N_EXPERTS * D_EXPERT))
    w2 = _bf(e_w2.reshape(N_EXPERTS * D_EXPERT, D_MODEL))
    return rwh, rwl, rb, w1, w3, w2


def kernel(x, ev_w_in, ev_mu, ev_w0, ev_w2, ev_a0, ev_a2, ev_g2, ev_k_k, ev_k_a, ev_r_k, ev_lnx_g, ev_lnx_b, ev_pool_w, ev_pool_scale, ev_w_out, od_w_in, od_conv_w, od_conv_b, od_cln_g, od_cln_b, od_gn_g, od_gn_b, od_w_out, ln_mix_g, ln_mix_b, rg_w, rg_b, re_w, re_b, e_w1, e_w3, e_w2, ln_ffn_g, ln_ffn_b):
    bsz, seq, _ = x.shape
    x2 = x.reshape(bsz * seq, D_MODEL)
    ones_bd = jnp.asarray(np.kron(np.eye(N_HEADS), np.ones((HEAD_DIM, HEAD_DIM))), BF16)
    expand = jnp.asarray(
        np.pad(np.kron(np.eye(N_EXPERTS), np.ones((1, D_EXPERT))), ((0, LANES - N_EXPERTS), (0, 0))), BF16)
    zeros_lora = jnp.zeros((LORA_W, HALF), F32)

    w2p = jnp.concatenate([ev_w2[0], zeros_lora], axis=0)
    a2p = jnp.concatenate([zeros_lora, ev_a2[0]], axis=0)
    r, lw, k, v, kk, a, g, bonus, ypool = _ev_front(
        x2, bsz, seq, _bf(ev_w_in[0]), _row(ev_mu[0]), _row(ev_w0[0]), _bf(w2p), _row(ev_a0[0]), _bf(a2p),
        _bf(ev_g2[0]), _row(ev_k_k[0]), _row(ev_k_a[0]), _row(ev_r_k[0]), ones_bd, _bf(ev_pool_w[0]),
        _row(ev_pool_scale[0]))
    y = _rwkv_scan(r, lw, k, v, kk, a, bsz, seq)
    x2 = _ev_back(y, g, bonus, ypool, x2, ones_bd, _row(ev_lnx_g[0]), _row(ev_lnx_b[0]), _bf(ev_w_out[0]),
                  _row(ln_mix_g[0]), _row(ln_mix_b[0]))
    x2 = _moe(x2, *_moe_weights(rg_w[0], rg_b[0], re_w[0], re_b[0], e_w1[0], e_w3[0], e_w2[0]), expand,
              _row(ln_ffn_g[0]), _row(ln_ffn_b[0]))

    cos_t, sin_t = _rope_table(seq)
    yconv, q, kr, vr, sg = _od_front(x2, bsz, seq, _bf(od_w_in[0]), od_conv_w[0], _row(od_conv_b[0]),
                                     _row(od_cln_g[0]), _row(od_cln_b[0]), cos_t, sin_t)
    yret = _retention(q, kr, vr, sg, bsz, seq, _row(od_gn_g[0]), _row(od_gn_b[0]))
    x2 = _od_back(yconv, yret, x2, _bf(od_w_out[0]), _row(ln_mix_g[1]), _row(ln_mix_b[1]))
    x2 = _moe(x2, *_moe_weights(rg_w[1], rg_b[1], re_w[1], re_b[1], e_w1[1], e_w3[1], e_w2[1]), expand,
              _row(ln_ffn_g[1]), _row(ln_ffn_b[1]))
    return x2.reshape(bsz, seq, D_MODEL)
```

```python
import functools
import math

import numpy as np
import jax
import jax.numpy as jnp
from jax import lax
from jax.experimental import pallas as pl
from jax.experimental.pallas import tpu as pltpu

F32 = jnp.float32
BF16 = jnp.bfloat16

D_MODEL = 1024
HALF = D_MODEL // 2
HEAD_DIM = 64
N_HEADS = HALF // HEAD_DIM
PAIR = 2 * HEAD_DIM
N_PAIRS = N_HEADS // 2
LORA_W = 64
LORA_A = 64
LORA_G = 128
RWKV_PROJ = 3 * HALF + LORA_W + LORA_A + LORA_G
EVEN_PROJ = RWKV_PROJ + HALF
ODD_PROJ = 6 * HALF
RWKV_NORM_EPS = 64e-5
LN_EPS = 1e-5
POOL_WINDOWS = (2, 4, 8, 16)
POOL_HALO = 16
CONV_WIDTH = 31
CONV_HALO = 32
ROPE_BASE = 10000.0
MOE_GROUPS = 4
EXPERTS_PER_GROUP = 8
N_EXPERTS = MOE_GROUPS * EXPERTS_PER_GROUP
D_EXPERT = 128
DEPTH = 2
ALPHA = (2.0 * DEPTH) ** 0.25
LANES = 128
SUBLANES = 8
NEG_BIG = -1e30

ROW_TILE = 512
SCAN_CHUNK = 64
SCAN_ROWS = 256
RET_CHUNK = 256
VMEM_LIMIT = 56 * 1024 * 1024


def _bf(v):
    return v.astype(BF16)


def _dot(a, b):
    return jnp.dot(a, b, preferred_element_type=F32)


def _dot_nt(a, b):
    return lax.dot_general(a, b, (((1,), (1,)), ((), ())), preferred_element_type=F32)


def _dot_tn(a, b):
    return lax.dot_general(a, b, (((0,), (0,)), ((), ())), preferred_element_type=F32)


def _dotb(a, b):
    return _dot(_bf(a), _bf(b))


def _split_dot(a, b_bf16):
    hi = _bf(a)
    lo = _bf(a - hi.astype(F32))
    return _dot(hi, b_bf16) + _dot(lo, b_bf16)


def _sigmoid(v):
    return 1.0 / (1.0 + jnp.exp(-v))


def _layer_norm(v, g, b, eps):
    mu = jnp.mean(v, axis=-1, keepdims=True)
    d = v - mu
    var = jnp.mean(d * d, axis=-1, keepdims=True)
    return d * lax.rsqrt(var + eps) * g + b


def _head_norm(v, ones_bd, eps):
    mu = _split_dot(v, ones_bd) * (1.0 / HEAD_DIM)
    d = v - mu
    var = _split_dot(d * d, ones_bd) * (1.0 / HEAD_DIM)
    return d * lax.rsqrt(var + eps)


def _full(shape):
    nd = len(shape)
    return pl.BlockSpec(shape, lambda *_: (0,) * nd)


def _params(sem):
    return pltpu.CompilerParams(dimension_semantics=sem, vmem_limit_bytes=VMEM_LIMIT)


def _ev_front_kernel(x_ref, win_ref, mu_ref, w0_ref, w2p_ref, a0_ref, a2p_ref, g2_ref, kkw_ref,
                     ka_ref, rk_ref, ones_ref, poolw_ref, pscale_ref,
                     r_o, lw_o, k_o, v_o, kk_o, a_o, g_o, bonus_o, ypool_o,
                     prow_s, ucarry_s):
    t = pl.program_id(1)
    tm = x_ref.shape[0]

    @pl.when(t == 0)
    def _():
        prow_s[...] = jnp.zeros_like(prow_s)
        ucarry_s[...] = jnp.zeros_like(ucarry_s)

    p = _dot(_bf(x_ref[...]), win_ref[...])
    pr = p[:, :RWKV_PROJ]
    u = p[:, RWKV_PROJ:]

    row = lax.broadcasted_iota(jnp.int32, (tm, RWKV_PROJ), 0)
    prev = jnp.where(row == 0, prow_s[0:1, :], pltpu.roll(pr, 1, 0))
    prow_s[0:1, :] = pr[tm - 1:tm, :]
    z = pr + mu_ref[...] * (prev - pr)

    r = z[:, 0:HALF]
    k = z[:, HALF:2 * HALF]
    v = z[:, 2 * HALF:3 * HALF]
    zl = z[:, 3 * HALF:3 * HALF + LORA_W + LORA_A]
    zg = z[:, 3 * HALF + LORA_W + LORA_A:RWKV_PROJ]

    yw = w0_ref[...] + _dotb(jnp.tanh(zl), w2p_ref[...])
    lw = -math.exp(-0.5) * _sigmoid(yw)
    a = _sigmoid(a0_ref[...] + _dotb(zl, a2p_ref[...]))
    g = _dotb(_sigmoid(zg), g2_ref[...])

    ones_bd = ones_ref[...]
    kk = k * kkw_ref[...]
    kk = kk * lax.rsqrt(jnp.maximum(_split_dot(kk * kk, ones_bd), 1e-24))
    kmod = k * (1.0 + (a - 1.0) * ka_ref[...])
    bonus = _split_dot(r * kmod * rk_ref[...], ones_bd) * v

    r_o[...] = r
    lw_o[...] = lw
    k_o[...] = kmod
    v_o[...] = v
    kk_o[...] = kk
    a_o[...] = a
    g_o[...] = g
    bonus_o[...] = bonus

    ext = jnp.concatenate([ucarry_s[...], u], axis=0)
    ucarry_s[...] = u[tm - POOL_HALO:, :]
    pos = t * tm + lax.broadcasted_iota(jnp.int32, (tm, LANES), 0)
    for gi, win in enumerate(POOL_WINDOWS):
        s = ext[:, gi * LANES:(gi + 1) * LANES]
        for step in range(gi + 1):
            s = s + pltpu.roll(s, 2 ** step, 0)
        count = jnp.minimum(pos + 1, win).astype(F32)
        u_g = u[:, gi * LANES:(gi + 1) * LANES]
        pooled = s[POOL_HALO:, :] / count - u_g
        ypool_o[:, gi * LANES:(gi + 1) * LANES] = (
            _dotb(pooled, poolw_ref[gi]) * pscale_ref[:, gi * LANES:(gi + 1) * LANES])


def _ev_front(x2, bsz, seq, win, mu, w0, w2p, a0, a2p, g2, kkw, ka, rk, ones_bd, poolw, pscale):
    tm = min(ROW_TILE, seq)
    nt = seq // tm
    n = bsz * seq
    row_spec = lambda w: pl.BlockSpec((tm, w), lambda b, t: (b * nt + t, 0))
    consts = (win, mu, w0, w2p, a0, a2p, g2, kkw, ka, rk, ones_bd, poolw, pscale)
    return pl.pallas_call(
        _ev_front_kernel,
        grid=(bsz, nt),
        in_specs=[row_spec(D_MODEL)] + [_full(c.shape) for c in consts],
        out_specs=[row_spec(HALF)] * 9,
        out_shape=[jax.ShapeDtypeStruct((n, HALF), F32)] * 9,
        scratch_shapes=[pltpu.VMEM((8, RWKV_PROJ), F32), pltpu.VMEM((POOL_HALO, HALF), F32)],
        compiler_params=_params(("arbitrary", "arbitrary")),
        name="ev_front",
    )(x2, *consts)


def _scan_masks(c):
    n = 2 * c
    i = np.arange(n)[:, None]
    j = np.arange(n)[None, :]
    same = (i // c) == (j // c)
    masks = [same & (i > j), same & (i >= j), (i == j), (i // 8 == j // 8) & (i > j)]
    b = 8
    while b < c:
        masks.append((i // (2 * b) == j // (2 * b)) & (i // b != j // b) & (i > j))
        b *= 2
    return np.stack(masks).astype(np.float32)


def _rwkv_scan_kernel(r_ref, lw_ref, k_ref, v_ref, kk_ref, a_ref, tri_ref, masks_ref, y_o, h_s):
    c = SCAN_CHUNK
    rows = r_ref.shape[0]
    n_chunks = rows // c
    n = 2 * c

    @pl.when(pl.program_id(1) == 0)
    def _():
        h_s[...] = jnp.zeros_like(h_s)

    lw = lw_ref[...]
    cl = _split_dot_lhs(tri_ref[...], lw)
    cl_last_rows = [cl[(g + 1) * c - 1:(g + 1) * c, :] for g in range(n_chunks)]
    cl_last = jnp.concatenate([jnp.broadcast_to(m, (c, HALF)) for m in cl_last_rows], axis=0)
    e_in = jnp.exp(cl)
    e_ex = jnp.exp(cl - lw)
    e_neg = jnp.exp(-cl)
    e_rem = jnp.exp(cl_last - cl)

    kk = kk_ref[...]
    kv = k_ref[...]
    beta = kk * a_ref[...]
    abar = -kk * e_ex
    rbar = r_ref[...] * e_in
    btil = beta * e_neg
    ktil = kv * e_neg
    bhat = beta * e_rem
    khat = kv * e_rem
    vv = v_ref[...]

    m_strict = masks_ref[0]
    m_incl = masks_ref[1]
    eye = masks_ref[2]
    m_blk = masks_ref[3]
    n_merge = masks_ref.shape[0] - 4

    lane = lax.broadcasted_iota(jnp.int32, (c, PAIR), 1)
    even = lane < HEAD_DIM

    def stack_bd(m):
        return jnp.concatenate([jnp.where(even, m, 0.0), jnp.where(even, 0.0, m)], axis=0)

    def stack_2(m):
        return jnp.concatenate([m, m], axis=0)

    items = [(g, pr) for g in range(n_chunks) for pr in range(N_PAIRS)]

    def cut(m, it):
        g, pr = it
        return m[g * c:(g + 1) * c, pr * PAIR:(pr + 1) * PAIR]

    abar_bd = [stack_bd(cut(abar, it)) for it in items]
    rbar_bd = [stack_bd(cut(rbar, it)) for it in items]
    v_bd16 = [_bf(stack_bd(cut(vv, it))) for it in items]
    bhat_bd16 = [_bf(stack_bd(cut(bhat, it))) for it in items]
    khat_bd16 = [_bf(stack_bd(cut(khat, it))) for it in items]
    sc = [_dot_nt(_bf(jnp.concatenate([abar_bd[i], rbar_bd[i]], axis=0)),
                  _bf(jnp.concatenate([stack_2(cut(btil, it)), stack_2(cut(ktil, it))], axis=0)))
          for i, it in enumerate(items)]
    a_ab = [m[:n, :n] * m_strict for m in sc]
    a_ak16 = [_bf(m[:n, n:] * m_strict) for m in sc]
    a_rb16 = [_bf(m[n:, :n] * m_incl) for m in sc]
    a_rk16 = [_bf(m[n:, n:] * m_incl) for m in sc]

    a_d = [m * m_blk for m in a_ab]
    pw = [_dotb(m, m) for m in a_d]
    tinv = [_dotb(eye + a_d[i], eye + pw[i]) for i in range(len(items))]
    pw = [_dotb(m, m) for m in pw]
    tinv = [_dotb(tinv[i], eye + pw[i]) for i in range(len(items))]
    for lvl in range(n_merge):
        m_off = masks_ref[4 + lvl]
        tinv16 = [_bf(m) for m in tinv]
        at = [_dot(_bf(a_ab[i] * m_off), tinv16[i]) for i in range(len(items))]
        tinv = [tinv[i] + _dot(tinv16[i], _bf(at[i])) for i in range(len(items))]

    akv = [_dot(a_ak16[i], v_bd16[i]) for i in range(len(items))]
    wu16 = [_bf(_dotb(tinv[i], jnp.concatenate([abar_bd[i], akv[i]], axis=1)))
            for i in range(len(items))]
    arb_wu = [_dot(a_rb16[i], wu16[i]) for i in range(len(items))]
    qhat16 = [_bf(rbar_bd[i] + arb_wu[i][:, :PAIR]) for i in range(len(items))]
    y0 = [arb_wu[i][:, PAIR:] + _dot(a_rk16[i], v_bd16[i]) for i in range(len(items))]
    mg = [_dot_tn(bhat_bd16[i], wu16[i]) for i in range(len(items))]
    m_mat16 = [_bf(m[:, :PAIR]) for m in mg]
    g_mat = [mg[i][:, PAIR:] + _dot_tn(khat_bd16[i], v_bd16[i]) for i in range(len(items))]

    h = [h_s[pr] for pr in range(N_PAIRS)]
    for i, (g, pr) in enumerate(items):
        h16 = _bf(h[pr])
        y_bd = _dot(qhat16[i], h16) + y0[i]
        y_o[g * c:(g + 1) * c, pr * PAIR:(pr + 1) * PAIR] = y_bd[:c, :] + y_bd[c:, :]
        p_last = jnp.exp(cl_last_rows[g][:, pr * PAIR:(pr + 1) * PAIR])
        p_col = jnp.sum(eye * p_last, axis=1, keepdims=True)
        h[pr] = p_col * h[pr] + _dot(m_mat16[i], h16) + g_mat[i]
    for pr in range(N_PAIRS):
        h_s[pr] = h[pr]


def _split_dot_lhs(a_bf16, b):
    hi = _bf(b)
    lo = _bf(b - hi.astype(F32))
    return _dot(a_bf16, hi) + _dot(a_bf16, lo)


def _rwkv_scan(r, lw, k, v, kk, a, bsz, seq):
    c = SCAN_CHUNK
    rows = min(SCAN_ROWS, seq)
    nb = seq // rows
    n = bsz * seq
    tri = jnp.asarray(np.kron(np.eye(rows // c), np.tril(np.ones((c, c)))), BF16)
    masks = jnp.asarray(_scan_masks(c))
    row_spec = pl.BlockSpec((rows, HALF), lambda b, t: (b * nb + t, 0))
    return pl.pallas_call(
        _rwkv_scan_kernel,
        grid=(bsz, nb),
        in_specs=[row_spec] * 6 + [_full(tri.shape), _full(masks.shape)],
        out_specs=row_spec,
        out_shape=jax.ShapeDtypeStruct((n, HALF), F32),
        scratch_shapes=[pltpu.VMEM((N_PAIRS, PAIR, PAIR), F32)],
        compiler_params=_params(("arbitrary", "arbitrary")),
        name="rwkv_scan",
    )(r, lw, k, v, kk, a, tri, masks)


def _ev_back_kernel(y_ref, g_ref, bonus_ref, ypool_ref, x_ref, ones_ref, lng_ref, lnb_ref,
                    wout_ref, mg_ref, mb_ref, o_ref):
    hn = _head_norm(y_ref[...], ones_ref[...], RWKV_NORM_EPS)
    y_rwkv = (hn * lng_ref[...] + lnb_ref[...] + bonus_ref[...]) * g_ref[...]
    h = _dot(_bf(y_rwkv), wout_ref[0:HALF, :]) + _dot(_bf(ypool_ref[...]), wout_ref[HALF:, :])
    o_ref[...] = _layer_norm(ALPHA * x_ref[...] + h, mg_ref[...], mb_ref[...], LN_EPS)


def _ev_back(y, g, bonus, ypool, x2, ones_bd, lng, lnb, wout, mg, mb):
    n = x2.shape[0]
    tm = min(ROW_TILE, n)
    half_spec = pl.BlockSpec((tm, HALF), lambda i: (i, 0))
    full_spec = pl.BlockSpec((tm, D_MODEL), lambda i: (i, 0))
    consts = (ones_bd, lng, lnb, wout, mg, mb)
    return pl.pallas_call(
        _ev_back_kernel,
        grid=(n // tm,),
        in_specs=[half_spec] * 4 + [full_spec] + [_full(c.shape) for c in consts],
        out_specs=full_spec,
        out_shape=jax.ShapeDtypeStruct((n, D_MODEL), F32),
        compiler_params=_params(("parallel",)),
        name="ev_back",
    )(y, g, bonus, ypool, x2, *consts)


MOE_TILE = 512
MOE_BLOCK = 512
SEG_ALIGN = 16
MOE_TILE_PAD = 640
GROUP_WIDTH = EXPERTS_PER_GROUP * D_EXPERT


def _pad_to(v, m):
    return ((v + (m - 1)) // m) * m


def _seg_layout(counts):
    padded = [_pad_to(c, SEG_ALIGN) for c in counts]
    starts = [0]
    for g in range(1, MOE_GROUPS):
        starts.append(starts[-1] + padded[g - 1])
    return padded, starts


def _moe_route_kernel(x_ref, rwh_ref, rwl_ref, rb_ref, tri_ref,
                      xs_hbm, gs_hbm, pos_o, meta_o,
                      xbuf_s, gbuf_s, zx_s, zg_s, cnt_s, pend_s, sem):
    t = pl.program_id(0)
    nt = pl.num_programs(0)
    tm = x_ref.shape[0]
    cap = xs_hbm.shape[0] // MOE_GROUPS
    slot = t % 2

    def seg_copies(sl, src_row, dst_row):
        src_row = pl.multiple_of(src_row, SEG_ALIGN)
        dst_row = pl.multiple_of(dst_row, SEG_ALIGN)
        return (pltpu.make_async_copy(xbuf_s.at[sl, pl.ds(src_row, SEG_ALIGN)],
                                      xs_hbm.at[pl.ds(dst_row, SEG_ALIGN)], sem.at[sl]),
                pltpu.make_async_copy(gbuf_s.at[sl, pl.ds(src_row, SEG_ALIGN)],
                                      gs_hbm.at[pl.ds(dst_row, SEG_ALIGN)], sem.at[sl]))

    def wait_slot(sl):
        def body(i, carry):
            for cp in seg_copies(sl, 0, 0):
                cp.wait()
            return carry
        lax.fori_loop(0, pend_s[sl], body, 0)
        pend_s[sl] = 0

    @pl.when(t == 0)
    def _():
        for g in range(MOE_GROUPS):
            cnt_s[g] = 0
        pend_s[0] = 0
        pend_s[1] = 0
        zx_s[...] = jnp.zeros_like(zx_s)
        zg_s[...] = jnp.zeros_like(zg_s)

    x = x_ref[...]
    xh = _bf(x)
    xl = _bf(x - xh.astype(F32))
    logits = (_dot(xh, rwh_ref[...]) + _dot(xl, rwh_ref[...]) + _dot(xh, rwl_ref[...])
              + rb_ref[...])
    lane = lax.broadcasted_iota(jnp.int32, logits.shape, 1)
    lanef = lane.astype(F32)
    is_g = (lane >= N_EXPERTS) & (lane < N_EXPERTS + MOE_GROUPS)
    gl = jnp.where(is_g, logits, NEG_BIG)
    gmax = jnp.max(gl, axis=1, keepdims=True)
    gidx = jnp.min(jnp.where(gl == gmax, lanef, 1e9), axis=1, keepdims=True) - float(N_EXPERTS)
    gden = jnp.sum(jnp.where(is_g, jnp.exp(gl - gmax), 0.0), axis=1, keepdims=True)
    g_w = 1.0 / gden
    grp = jnp.floor(lanef * (1.0 / EXPERTS_PER_GROUP))
    sel = (lane < N_EXPERTS) & (grp == gidx)
    el = jnp.where(sel, logits, NEG_BIG)
    m1 = jnp.max(el, axis=1, keepdims=True)
    i1 = jnp.min(jnp.where(el == m1, lanef, 1e9), axis=1, keepdims=True)
    el2 = jnp.where(lanef == i1, NEG_BIG, el)
    m2 = jnp.max(el2, axis=1, keepdims=True)
    i2 = jnp.min(jnp.where(el2 == m2, lanef, 1e9), axis=1, keepdims=True)
    e21 = jnp.exp(m2 - m1)
    w_top = g_w / (1.0 + e21)
    w_sec = g_w * e21 / (1.0 + e21)
    base = gidx * float(EXPERTS_PER_GROUP)
    gates8 = jnp.where(lanef == i1 - base, w_top, jnp.where(lanef == i2 - base, w_sec, 0.0))

    onehot = jnp.where(lanef == gidx, 1.0, 0.0)
    counts = [jnp.sum(onehot[:, g:g + 1]).astype(jnp.int32) for g in range(MOE_GROUPS)]
    padded, starts = _seg_layout(counts)
    rank = _dot(tri_ref[...], _bf(onehot))
    start_row = jnp.zeros((1, LANES), F32)
    for g in range(MOE_GROUPS):
        start_row = jnp.where(lane[0:1, :] == g, jnp.asarray(starts[g], jnp.int32).astype(F32), start_row)
    pos = jnp.sum(onehot * (rank + start_row), axis=1, keepdims=True)
    pos_o[...] = pos
    perm_t = jnp.where(pos == lax.broadcasted_iota(jnp.int32, (tm, MOE_TILE_PAD), 1).astype(F32),
                       1.0, 0.0).astype(BF16)

    wait_slot(slot)
    xbuf_s[slot] = _dot_tn(perm_t, xh).astype(BF16)
    g_hi = _bf(gates8)
    g_lo = _bf(gates8 - g_hi.astype(F32))
    gbuf_s[slot] = _dot_tn(perm_t, g_hi) + _dot_tn(perm_t, g_lo)

    n_issued = 0
    for g in range(MOE_GROUPS):
        dst0 = g * cap + cnt_s[g]
        meta_o[t * 2 * MOE_GROUPS + g] = dst0
        meta_o[t * 2 * MOE_GROUPS + MOE_GROUPS + g] = padded[g]
        nch = padded[g] // SEG_ALIGN

        def issue(i, carry, g=g, dst0=dst0):
            for cp in seg_copies(slot, starts[g] + i * SEG_ALIGN, dst0 + i * SEG_ALIGN):
                cp.start()
            return carry
        lax.fori_loop(0, nch, issue, 0)
        cnt_s[g] = cnt_s[g] + padded[g]
        n_issued = n_issued + nch
    pend_s[slot] = n_issued

    @pl.when(t == nt - 1)
    def _():
        wait_slot(0)
        wait_slot(1)
        for g in range(MOE_GROUPS):
            dst0 = g * cap + cnt_s[g]

            def zero_copies(i, dst0=dst0):
                row = pl.multiple_of(dst0 + i * SEG_ALIGN, SEG_ALIGN)
                return (pltpu.make_async_copy(zx_s, xs_hbm.at[pl.ds(row, SEG_ALIGN)], sem.at[0]),
                        pltpu.make_async_copy(zg_s, gs_hbm.at[pl.ds(row, SEG_ALIGN)], sem.at[0]))

            def zissue(i, carry):
                for cp in zero_copies(i):
                    cp.start()
                return carry

            def zwait(i, carry):
                for cp in zero_copies(i):
                    cp.wait()
                return carry
            lax.fori_loop(0, MOE_BLOCK // SEG_ALIGN, zissue, 0)
            lax.fori_loop(0, MOE_BLOCK // SEG_ALIGN, zwait, 0)


def _moe_route(x2, rwh, rwl, rb):
    n = x2.shape[0]
    tm = MOE_TILE
    nt = n // tm
    cap = _pad_to(n + nt * SEG_ALIGN + MOE_BLOCK, MOE_BLOCK)
    tri = jnp.asarray(np.tril(np.ones((tm, tm)), -1), BF16)
    any_spec = pl.BlockSpec(memory_space=pl.ANY)
    return pl.pallas_call(
        _moe_route_kernel,
        grid=(nt,),
        in_specs=[pl.BlockSpec((tm, D_MODEL), lambda i: (i, 0)), _full(rwh.shape), _full(rwl.shape),
                  _full(rb.shape), _full(tri.shape)],
        out_specs=[any_spec, any_spec, pl.BlockSpec((tm, 1), lambda i: (i, 0)),
                   pl.BlockSpec(memory_space=pltpu.SMEM)],
        out_shape=[jax.ShapeDtypeStruct((MOE_GROUPS * cap, D_MODEL), BF16),
                   jax.ShapeDtypeStruct((MOE_GROUPS * cap, LANES), F32),
                   jax.ShapeDtypeStruct((n, 1), F32),
                   jax.ShapeDtypeStruct((nt * 2 * MOE_GROUPS,), jnp.int32)],
        scratch_shapes=[pltpu.VMEM((2, MOE_TILE_PAD, D_MODEL), BF16), pltpu.VMEM((2, MOE_TILE_PAD, LANES), F32),
                        pltpu.VMEM((SEG_ALIGN, D_MODEL), BF16), pltpu.VMEM((SEG_ALIGN, LANES), F32),
                        pltpu.SMEM((MOE_GROUPS,), jnp.int32), pltpu.SMEM((2,), jnp.int32),
                        pltpu.SemaphoreType.DMA((2,))],
        compiler_params=_params(("arbitrary",)),
        name="moe_route",
    )(x2, rwh, rwl, rb, tri)


def _moe_experts_kernel(brow_ref, bgrp_ref, nval_ref, xs_ref, gs_ref, w1_ref, w3_ref, w2_ref, ex_ref, ys_ref):
    b = pl.program_id(0)

    @pl.when(b < nval_ref[0])
    def _():
        xb = xs_ref[...]
        h1 = _dot(xb, w1_ref[0])
        h3 = _dot(xb, w3_ref[0])
        gexp = _split_dot(gs_ref[...], ex_ref[...])
        act = h1 * _sigmoid(h1) * h3 * gexp
        ys_ref[...] = _dot(_bf(act), w2_ref[0])

    @pl.when(b >= nval_ref[0])
    def _():
        ys_ref[...] = jnp.zeros_like(ys_ref)


def _moe_experts(xs, gs, brow, bgrp, nval, w1g, w3g, w2g, expand8):
    rows = xs.shape[0]
    nb = brow.shape[0]
    rb = MOE_BLOCK
    trash = rows // rb

    def out_map(b, brow_ref, bgrp_ref, nval_ref):
        return (jnp.where(b < nval_ref[0], brow_ref[b], trash), 0)

    grid_spec = pltpu.PrefetchScalarGridSpec(
        num_scalar_prefetch=3,
        grid=(nb,),
        in_specs=[pl.BlockSpec((rb, D_MODEL), lambda b, br, bg, nv: (br[b], 0)),
                  pl.BlockSpec((rb, LANES), lambda b, br, bg, nv: (br[b], 0)),
                  pl.BlockSpec((1, D_MODEL, GROUP_WIDTH), lambda b, br, bg, nv: (bg[b], 0, 0)),
                  pl.BlockSpec((1, D_MODEL, GROUP_WIDTH), lambda b, br, bg, nv: (bg[b], 0, 0)),
                  pl.BlockSpec((1, GROUP_WIDTH, D_MODEL), lambda b, br, bg, nv: (bg[b], 0, 0)),
                  pl.BlockSpec(expand8.shape, lambda b, br, bg, nv: (0, 0))],
        out_specs=pl.BlockSpec((rb, D_MODEL), out_map),
    )
    return pl.pallas_call(
        _moe_experts_kernel,
        grid_spec=grid_spec,
        out_shape=jax.ShapeDtypeStruct((rows + rb, D_MODEL), F32),
        compiler_params=_params(("arbitrary",)),
        name="moe_experts",
    )(brow, bgrp, nval, xs, gs, w1g, w3g, w2g, expand8)


def _moe_combine_kernel(meta_ref, ys_hbm, x_ref, pos_ref, fg_ref, fb_ref, o_ref, ybuf_s, sem):
    t = pl.program_id(0)
    nt = pl.num_programs(0)
    tm = x_ref.shape[0]
    slot = t % 2

    def tile_segments(tt):
        dst = [meta_ref[tt * 2 * MOE_GROUPS + g] for g in range(MOE_GROUPS)]
        padded = [meta_ref[tt * 2 * MOE_GROUPS + MOE_GROUPS + g] for g in range(MOE_GROUPS)]
        starts = [0]
        for g in range(1, MOE_GROUPS):
            starts.append(starts[-1] + padded[g - 1])
        return dst, padded, starts

    def seg_copy(sl, src_row, dst_row):
        src_row = pl.multiple_of(src_row, SEG_ALIGN)
        dst_row = pl.multiple_of(dst_row, SEG_ALIGN)
        return pltpu.make_async_copy(ys_hbm.at[pl.ds(src_row, SEG_ALIGN)],
                                     ybuf_s.at[sl, pl.ds(dst_row, SEG_ALIGN)], sem.at[sl])

    def fetch(tt, sl):
        dst, padded, starts = tile_segments(tt)
        for g in range(MOE_GROUPS):
            def body(i, carry, g=g):
                seg_copy(sl, dst[g] + i * SEG_ALIGN, starts[g] + i * SEG_ALIGN).start()
                return carry
            lax.fori_loop(0, padded[g] // SEG_ALIGN, body, 0)

    def wait_tile(tt, sl):
        _, padded, _ = tile_segments(tt)
        total = padded[0] + padded[1] + padded[2] + padded[3]

        def body(i, carry):
            seg_copy(sl, 0, 0).wait()
            return carry
        lax.fori_loop(0, total // SEG_ALIGN, body, 0)

    @pl.when(t == 0)
    def _():
        ybuf_s[...] = jnp.zeros_like(ybuf_s)
        fetch(0, 0)

    @pl.when(t + 1 < nt)
    def _():
        fetch(t + 1, 1 - slot)

    wait_tile(t, slot)
    perm_t = jnp.where(pos_ref[...] == lax.broadcasted_iota(jnp.int32, (tm, MOE_TILE_PAD), 1).astype(F32),
                       1.0, 0.0).astype(BF16)
    f = _split_dot_lhs(perm_t, ybuf_s[slot])
    o_ref[...] = _layer_norm(ALPHA * x_ref[...] + f, fg_ref[...], fb_ref[...], LN_EPS)


def _moe_combine(ys, x2, pos, meta, fg, fb):
    n = x2.shape[0]
    tm = MOE_TILE
    grid_spec = pltpu.PrefetchScalarGridSpec(
        num_scalar_prefetch=1,
        grid=(n // tm,),
        in_specs=[pl.BlockSpec(memory_space=pl.ANY),
                  pl.BlockSpec((tm, D_MODEL), lambda i, m: (i, 0)),
                  pl.BlockSpec((tm, 1), lambda i, m: (i, 0)),
                  pl.BlockSpec(fg.shape, lambda i, m: (0, 0)),
                  pl.BlockSpec(fb.shape, lambda i, m: (0, 0))],
        out_specs=pl.BlockSpec((tm, D_MODEL), lambda i, m: (i, 0)),
        scratch_shapes=[pltpu.VMEM((2, MOE_TILE_PAD, D_MODEL), F32), pltpu.SemaphoreType.DMA((2,))],
    )
    return pl.pallas_call(
        _moe_combine_kernel,
        grid_spec=grid_spec,
        out_shape=jax.ShapeDtypeStruct((n, D_MODEL), F32),
        compiler_params=_params(("arbitrary",)),
        name="moe_combine",
    )(meta, ys, x2, pos, fg, fb)


def _moe(x2, rwh, rwl, rb, w1g, w3g, w2g, expand8, fg, fb):
    n = x2.shape[0]
    xs, gs, pos, meta = _moe_route(x2, rwh, rwl, rb)
    cap = xs.shape[0] // MOE_GROUPS
    last = meta[-2 * MOE_GROUPS:]
    group_rows = last[:MOE_GROUPS] + last[MOE_GROUPS:] - jnp.arange(MOE_GROUPS, dtype=jnp.int32) * cap
    nblk = (group_rows + (MOE_BLOCK - 1)) // MOE_BLOCK
    ends = jnp.cumsum(nblk)
    max_rows = n + (n // MOE_TILE) * MOE_GROUPS * (SEG_ALIGN - 1)
    nb_max = max_rows // MOE_BLOCK + MOE_GROUPS + 1
    b = jnp.arange(nb_max, dtype=jnp.int32)
    bgrp = jnp.minimum(jnp.sum((b[:, None] >= ends[None, :]).astype(jnp.int32), axis=1), MOE_GROUPS - 1)
    brow = bgrp * (cap // MOE_BLOCK) + (b - (ends - nblk)[bgrp])
    nval = ends[-1:]
    last_valid = jnp.maximum(nval[0] - 1, 0)
    brow = jnp.where(b < nval[0], brow, brow[last_valid]).astype(jnp.int32)
    bgrp = jnp.where(b < nval[0], bgrp, bgrp[last_valid]).astype(jnp.int32)
    ys = _moe_experts(xs, gs, brow, bgrp, nval.astype(jnp.int32), w1g, w3g, w2g, expand8)
    return _moe_combine(ys, x2, pos, meta, fg, fb)


def _rope_table_kernel(cos_o, sin_o):
    tm = cos_o.shape[0]
    half = HEAD_DIM // 2
    pos = (pl.program_id(0) * tm + lax.broadcasted_iota(jnp.int32, (tm, LANES), 0)).astype(F32)
    lane = lax.broadcasted_iota(jnp.int32, (tm, LANES), 1)
    idx = (lane & (half - 1)).astype(F32)
    inv = jnp.exp(idx * (-math.log(ROPE_BASE) / half))
    ang = pos * inv
    first = (lane & half) == 0
    c = jnp.cos(ang)
    s = jnp.sin(ang)
    s = jnp.where(first, -s, s)
    for q in range(HALF // LANES):
        cos_o[:, q * LANES:(q + 1) * LANES] = c
        sin_o[:, q * LANES:(q + 1) * LANES] = s


def _rope_table(seq):
    tm = min(ROW_TILE, seq)
    spec = pl.BlockSpec((tm, HALF), lambda i: (i, 0))
    return pl.pallas_call(
        _rope_table_kernel,
        grid=(seq // tm,),
        in_specs=[],
        out_specs=[spec, spec],
        out_shape=[jax.ShapeDtypeStruct((seq, HALF), F32)] * 2,
        compiler_params=_params(("parallel",)),
        name="rope_table",
    )()


def _od_front_kernel(x_ref, win_ref, cw_ref, cb_ref, clg_ref, clb_ref, cos_ref, sin_ref,
                     yconv_o, q_o, k_o, v_o, sg_o, ubuf_s):
    t = pl.program_id(1)
    tm = x_ref.shape[0]

    @pl.when(t == 0)
    def _():
        ubuf_s[0:CONV_HALO, :] = jnp.zeros((CONV_HALO, HALF), F32)

    p = _dot(_bf(x_ref[...]), win_ref[...])
    ca = p[:, 0:HALF]
    cb = p[:, HALF:2 * HALF]
    q = p[:, 2 * HALF:3 * HALF]
    k = p[:, 3 * HALF:4 * HALF]
    v = p[:, 4 * HALF:5 * HALF]
    gr = p[:, 5 * HALF:6 * HALF]

    ubuf_s[CONV_HALO:, :] = ca * _sigmoid(cb)
    acc = jnp.zeros((tm, HALF), F32) + cb_ref[...]
    for rho in range(SUBLANES):
        ext = 0 if rho == 0 else SUBLANES
        part = None
        for j in range(CONV_WIDTH):
            off = CONV_HALO - (CONV_WIDTH - 1) + j
            if off % SUBLANES != rho:
                continue
            base = off - rho
            term = cw_ref[j:j + 1, :] * ubuf_s[base:base + tm + ext, :]
            part = term if part is None else part + term
        if rho == 0:
            acc = acc + part
        else:
            acc = acc + pltpu.roll(part, tm + ext - rho, 0)[0:tm, :]
    ubuf_s[0:CONV_HALO, :] = ubuf_s[tm:tm + CONV_HALO, :]
    ln = _layer_norm(acc, clg_ref[...], clb_ref[...], LN_EPS)
    yconv_o[...] = ln * _sigmoid(ln)

    lane = lax.broadcasted_iota(jnp.int32, (tm, HALF), 1)
    first = (lane & (HEAD_DIM // 2)) == 0
    cos = cos_ref[...]
    sin = sin_ref[...]

    def rot(m):
        partner = jnp.where(first, pltpu.roll(m, HALF - HEAD_DIM // 2, 1), pltpu.roll(m, HEAD_DIM // 2, 1))
        return m * cos + partner * sin

    q_o[...] = _bf(rot(q))
    k_o[...] = _bf(rot(k) * (HEAD_DIM ** -0.5))
    v_o[...] = _bf(v)
    sg_o[...] = gr * _sigmoid(gr)


def _od_front(x2, bsz, seq, win, cw, cb, clg, clb, cos_t, sin_t):
    tm = min(ROW_TILE, seq)
    nt = seq // tm
    n = bsz * seq
    row_spec = lambda w: pl.BlockSpec((tm, w), lambda b, t: (b * nt + t, 0))
    tab_spec = pl.BlockSpec((tm, HALF), lambda b, t: (t, 0))
    consts = (win, cw, cb, clg, clb)
    sds = lambda dt: jax.ShapeDtypeStruct((n, HALF), dt)
    return pl.pallas_call(
        _od_front_kernel,
        grid=(bsz, nt),
        in_specs=[row_spec(D_MODEL)] + [_full(c.shape) for c in consts] + [tab_spec, tab_spec],
        out_specs=[row_spec(HALF)] * 5,
        out_shape=[sds(F32), sds(BF16), sds(BF16), sds(BF16), sds(F32)],
        scratch_shapes=[pltpu.VMEM((tm + CONV_HALO, HALF), F32)],
        compiler_params=_params(("arbitrary", "arbitrary")),
        name="od_front",
    )(x2, *consts, cos_t, sin_t)


def _ret_consts(c):
    h = np.arange(N_HEADS, dtype=np.float64)
    log_gamma = np.log1p(-np.power(2.0, -5.0 - h))
    idx = np.arange(c, dtype=np.float64)
    diff = idx[:, None] - idx[None, :]
    dmask = np.where(diff >= 0, np.exp(np.maximum(diff, 0.0)[None] * log_gamma[:, None, None]), 0.0)
    xi = np.exp((idx + 1.0)[:, None] * log_gamma[None, :])
    zeta = np.exp((c - 1.0 - idx)[:, None] * log_gamma[None, :])
    xi = np.repeat(xi, HEAD_DIM, axis=1)
    zeta = np.repeat(zeta, HEAD_DIM, axis=1)
    gamma_c = np.exp(c * log_gamma)
    lane_head = np.arange(PAIR) // HEAD_DIM
    gdiag = np.zeros((N_PAIRS, PAIR, PAIR))
    for pr in range(N_PAIRS):
        same = lane_head[:, None] == lane_head[None, :]
        gdiag[pr] = np.where(same, gamma_c[2 * pr + lane_head][:, None], 0.0)
    bd = (lane_head[:, None] == lane_head[None, :]).astype(np.float32)
    return (dmask.astype(np.float32), xi.astype(np.float32), zeta.astype(np.float32),
            gdiag.astype(np.float32), bd)


def _retention_kernel(q_ref, k_ref, v_ref, sg_ref, dmask_ref, xi_ref, zeta_ref, gdiag_ref, bd_ref,
                      ones_ref, gng_ref, gnb_ref, o_ref, r_s):
    c = q_ref.shape[0]

    @pl.when(pl.program_id(1) == 0)
    def _():
        r_s[...] = jnp.zeros_like(r_s)

    lane = lax.broadcasted_iota(jnp.int32, (c, PAIR), 1)
    even = lane < HEAD_DIM
    bd = bd_ref[...]
    for pr in range(N_PAIRS):
        sl = slice(pr * PAIR, (pr + 1) * PAIR)
        qp = q_ref[:, sl]
        kp = k_ref[:, sl]
        vp = v_ref[:, sl]
        qf = qp.astype(F32)
        zero = jnp.zeros_like(qp)
        s_even = _dot_nt(jnp.where(even, qp, zero), kp) * dmask_ref[2 * pr]
        s_odd = _dot_nt(jnp.where(even, zero, qp), kp) * dmask_ref[2 * pr + 1]
        intra = jnp.where(even, _dot(_bf(s_even), vp), _dot(_bf(s_odd), vp))
        r0 = r_s[pr]
        cross = _dot(_bf(qf * xi_ref[:, sl]), _bf(r0))
        kz = _bf(kp.astype(F32) * zeta_ref[:, sl])
        r_s[pr] = gdiag_ref[pr] * r0 + bd * _dot_tn(kz, vp)
        ret = _head_norm(intra + cross, ones_ref[...], LN_EPS)
        o_ref[:, sl] = sg_ref[:, sl] * (ret * gng_ref[:, sl] + gnb_ref[:, sl])


def _retention(q, k, v, sg, bsz, seq, gng, gnb):
    c = min(RET_CHUNK, seq)
    nc = seq // c
    n = bsz * seq
    dmask, xi, zeta, gdiag, bd = (jnp.asarray(m) for m in _ret_consts(c))
    ones_pair = jnp.asarray(np.kron(np.eye(2), np.ones((HEAD_DIM, HEAD_DIM))), BF16)
    row_spec = pl.BlockSpec((c, HALF), lambda b, t: (b * nc + t, 0))
    consts = (dmask, xi, zeta, gdiag, bd, ones_pair, gng, gnb)
    return pl.pallas_call(
        _retention_kernel,
        grid=(bsz, nc),
        in_specs=[row_spec] * 4 + [_full(m.shape) for m in consts],
        out_specs=row_spec,
        out_shape=jax.ShapeDtypeStruct((n, HALF), F32),
        scratch_shapes=[pltpu.VMEM((N_PAIRS, PAIR, PAIR), F32)],
        compiler_params=_params(("arbitrary", "arbitrary")),
        name="retention",
    )(q, k, v, sg, *consts)


def _od_back_kernel(ya_ref, yb_ref, x_ref, wout_ref, mg_ref, mb_ref, o_ref):
    h = _dot(_bf(ya_ref[...]), wout_ref[0:HALF, :]) + _dot(_bf(yb_ref[...]), wout_ref[HALF:, :])
    o_ref[...] = _layer_norm(ALPHA * x_ref[...] + h, mg_ref[...], mb_ref[...], LN_EPS)


def _od_back(ya, yb, x2, wout, mg, mb):
    n = x2.shape[0]
    tm = min(ROW_TILE, n)
    half_spec = pl.BlockSpec((tm, HALF), lambda i: (i, 0))
    full_spec = pl.BlockSpec((tm, D_MODEL), lambda i: (i, 0))
    consts = (wout, mg, mb)
    return pl.pallas_call(
        _od_back_kernel,
        grid=(n // tm,),
        in_specs=[half_spec, half_spec, full_spec] + [_full(c.shape) for c in consts],
        out_specs=full_spec,
        out_shape=jax.ShapeDtypeStruct((n, D_MODEL), F32),
        compiler_params=_params(("parallel",)),
        name="od_back",
    )(ya, yb, x2, *consts)


def _row(v):
    return v.reshape(1, -1).astype(F32)


def _moe_weights(rg_w, rg_b, re_w, re_b, e_w1, e_w3, e_w2):
    pad = LANES - N_EXPERTS - MOE_GROUPS
    rw = jnp.concatenate([re_w, rg_w, jnp.zeros((D_MODEL, pad), F32)], axis=1)
    rwh = _bf(rw)
    rwl = _bf(rw - rwh.astype(F32))
    rb = jnp.concatenate([re_b, rg_b, jnp.zeros((pad,), F32)]).reshape(1, LANES)
    def up(w):
        return _bf(w.reshape(MOE_GROUPS, EXPERTS_PER_GROUP, D_MODEL, D_EXPERT).transpose(0, 2, 1, 3)
                   .reshape(MOE_GROUPS, D_MODEL, GROUP_WIDTH))
    w2 = _bf(e_w2.reshape(MOE_GROUPS, GROUP_WIDTH, D_MODEL))
    return rwh, rwl, rb, up(e_w1), up(e_w3), w2


def kernel(x, ev_w_in, ev_mu, ev_w0, ev_w2, ev_a0, ev_a2, ev_g2, ev_k_k, ev_k_a, ev_r_k, ev_lnx_g, ev_lnx_b, ev_pool_w, ev_pool_scale, ev_w_out, od_w_in, od_conv_w, od_conv_b, od_cln_g, od_cln_b, od_gn_g, od_gn_b, od_w_out, ln_mix_g, ln_mix_b, rg_w, rg_b, re_w, re_b, e_w1, e_w3, e_w2, ln_ffn_g, ln_ffn_b):
    bsz, seq, _ = x.shape
    x2 = x.reshape(bsz * seq, D_MODEL)
    ones_bd = jnp.asarray(np.kron(np.eye(N_HEADS), np.ones((HEAD_DIM, HEAD_DIM))), BF16)
    expand = jnp.asarray(
        np.pad(np.kron(np.eye(EXPERTS_PER_GROUP), np.ones((1, D_EXPERT))),
               ((0, LANES - EXPERTS_PER_GROUP), (0, 0))), BF16)
    zeros_lora = jnp.zeros((LORA_W, HALF), F32)

    w2p = jnp.concatenate([ev_w2[0], zeros_lora], axis=0)
    a2p = jnp.concatenate([zeros_lora, ev_a2[0]], axis=0)
    r, lw, k, v, kk, a, g, bonus, ypool = _ev_front(
        x2, bsz, seq, _bf(ev_w_in[0]), _row(ev_mu[0]), _row(ev_w0[0]), _bf(w2p), _row(ev_a0[0]), _bf(a2p),
        _bf(ev_g2[0]), _row(ev_k_k[0]), _row(ev_k_a[0]), _row(ev_r_k[0]), ones_bd, _bf(ev_pool_w[0]),
        _row(ev_pool_scale[0]))
    y = _rwkv_scan(r, lw, k, v, kk, a, bsz, seq)
    x2 = _ev_back(y, g, bonus, ypool, x2, ones_bd, _row(ev_lnx_g[0]), _row(ev_lnx_b[0]), _bf(ev_w_out[0]),
                  _row(ln_mix_g[0]), _row(ln_mix_b[0]))
    x2 = _moe(x2, *_moe_weights(rg_w[0], rg_b[0], re_w[0], re_b[0], e_w1[0], e_w3[0], e_w2[0]), expand,
              _row(ln_ffn_g[0]), _row(ln_ffn_b[0]))

    cos_t, sin_t = _rope_table(seq)
    yconv, q, kr, vr, sg = _od_front(x2, bsz, seq, _bf(od_w_in[0]), od_conv_w[0], _row(od_conv_b[0]),
                                     _row(od_cln_g[0]), _row(od_cln_b[0]), cos_t, sin_t)
    yret = _retention(q, kr, vr, sg, bsz, seq, _row(od_gn_g[0]), _row(od_gn_b[0]))
    x2 = _od_back(yconv, yret, x2, _bf(od_w_out[0]), _row(ln_mix_g[1]), _row(ln_mix_b[1]))
    x2 = _moe(x2, *_moe_weights(rg_w[1], rg_b[1], re_w[1], re_b[1], e_w1[1], e_w3[1], e_w2[1]), expand,
              _row(ln_ffn_g[1]), _row(ln_ffn_b[1]))
    return x2.reshape(bsz, seq, D_MODEL)
```

```python
import functools
import math

import numpy as np
import jax
import jax.numpy as jnp
from jax import lax
from jax.experimental import pallas as pl
from jax.experimental.pallas import tpu as pltpu

F32 = jnp.float32
BF16 = jnp.bfloat16

D_MODEL = 1024
HALF = D_MODEL // 2
HEAD_DIM = 64
N_HEADS = HALF // HEAD_DIM
PAIR = 2 * HEAD_DIM
N_PAIRS = N_HEADS // 2
LORA_W = 64
LORA_A = 64
LORA_G = 128
RWKV_PROJ = 3 * HALF + LORA_W + LORA_A + LORA_G
EVEN_PROJ = RWKV_PROJ + HALF
ODD_PROJ = 6 * HALF
RWKV_NORM_EPS = 64e-5
LN_EPS = 1e-5
POOL_WINDOWS = (2, 4, 8, 16)
POOL_HALO = 16
CONV_WIDTH = 31
CONV_HALO = 32
ROPE_BASE = 10000.0
MOE_GROUPS = 4
EXPERTS_PER_GROUP = 8
N_EXPERTS = MOE_GROUPS * EXPERTS_PER_GROUP
D_EXPERT = 128
DEPTH = 2
ALPHA = (2.0 * DEPTH) ** 0.25
LANES = 128
SUBLANES = 8
NEG_BIG = -1e30

ROW_TILE = 512
SCAN_CHUNK = 64
SCAN_ROWS = 256
RET_CHUNK = 256
VMEM_LIMIT = 56 * 1024 * 1024


def _bf(v):
    return v.astype(BF16)


def _dot(a, b):
    return jnp.dot(a, b, preferred_element_type=F32)


def _dot_nt(a, b):
    return lax.dot_general(a, b, (((1,), (1,)), ((), ())), preferred_element_type=F32)


def _dot_tn(a, b):
    return lax.dot_general(a, b, (((0,), (0,)), ((), ())), preferred_element_type=F32)


def _dotb(a, b):
    return _dot(_bf(a), _bf(b))


def _split_dot(a, b_bf16):
    hi = _bf(a)
    lo = _bf(a - hi.astype(F32))
    return _dot(hi, b_bf16) + _dot(lo, b_bf16)


def _sigmoid(v):
    return 1.0 / (1.0 + jnp.exp(-v))


def _layer_norm(v, g, b, eps):
    mu = jnp.mean(v, axis=-1, keepdims=True)
    d = v - mu
    var = jnp.mean(d * d, axis=-1, keepdims=True)
    return d * lax.rsqrt(var + eps) * g + b


def _head_norm(v, ones_bd, eps):
    mu = _split_dot(v, ones_bd) * (1.0 / HEAD_DIM)
    d = v - mu
    var = _split_dot(d * d, ones_bd) * (1.0 / HEAD_DIM)
    return d * lax.rsqrt(var + eps)


def _full(shape):
    nd = len(shape)
    return pl.BlockSpec(shape, lambda *_: (0,) * nd)


def _params(sem):
    return pltpu.CompilerParams(dimension_semantics=sem, vmem_limit_bytes=VMEM_LIMIT)


def _ev_front_kernel(x_ref, win_ref, mu_ref, w0_ref, w2p_ref, a0_ref, a2p_ref, g2_ref, kkw_ref,
                     ka_ref, rk_ref, ones_ref, poolw_ref, pscale_ref,
                     r_o, lw_o, k_o, v_o, kk_o, a_o, g_o, bonus_o, ypool_o,
                     prow_s, ucarry_s):
    t = pl.program_id(1)
    tm = x_ref.shape[0]

    @pl.when(t == 0)
    def _():
        prow_s[...] = jnp.zeros_like(prow_s)
        ucarry_s[...] = jnp.zeros_like(ucarry_s)

    p = _dot(_bf(x_ref[...]), win_ref[...])
    pr = p[:, :RWKV_PROJ]
    u = p[:, RWKV_PROJ:]

    row = lax.broadcasted_iota(jnp.int32, (tm, RWKV_PROJ), 0)
    prev = jnp.where(row == 0, prow_s[0:1, :], pltpu.roll(pr, 1, 0))
    prow_s[0:1, :] = pr[tm - 1:tm, :]
    z = pr + mu_ref[...] * (prev - pr)

    r = z[:, 0:HALF]
    k = z[:, HALF:2 * HALF]
    v = z[:, 2 * HALF:3 * HALF]
    zl = z[:, 3 * HALF:3 * HALF + LORA_W + LORA_A]
    zg = z[:, 3 * HALF + LORA_W + LORA_A:RWKV_PROJ]

    yw = w0_ref[...] + _dotb(jnp.tanh(zl), w2p_ref[...])
    lw = -math.exp(-0.5) * _sigmoid(yw)
    a = _sigmoid(a0_ref[...] + _dotb(zl, a2p_ref[...]))
    g = _dotb(_sigmoid(zg), g2_ref[...])

    ones_bd = ones_ref[...]
    kk = k * kkw_ref[...]
    kk = kk * lax.rsqrt(jnp.maximum(_split_dot(kk * kk, ones_bd), 1e-24))
    kmod = k * (1.0 + (a - 1.0) * ka_ref[...])
    bonus = _split_dot(r * kmod * rk_ref[...], ones_bd) * v

    r_o[...] = r
    lw_o[...] = lw
    k_o[...] = kmod
    v_o[...] = v
    kk_o[...] = kk
    a_o[...] = a
    g_o[...] = g
    bonus_o[...] = bonus

    ext = jnp.concatenate([ucarry_s[...], u], axis=0)
    ucarry_s[...] = u[tm - POOL_HALO:, :]
    pos = t * tm + lax.broadcasted_iota(jnp.int32, (tm, LANES), 0)
    for gi, win in enumerate(POOL_WINDOWS):
        s = ext[:, gi * LANES:(gi + 1) * LANES]
        for step in range(gi + 1):
            s = s + pltpu.roll(s, 2 ** step, 0)
        count = jnp.minimum(pos + 1, win).astype(F32)
        u_g = u[:, gi * LANES:(gi + 1) * LANES]
        pooled = s[POOL_HALO:, :] / count - u_g
        ypool_o[:, gi * LANES:(gi + 1) * LANES] = _bf(
            _dotb(pooled, poolw_ref[gi]) * pscale_ref[:, gi * LANES:(gi + 1) * LANES])


def _ev_front(x2, bsz, seq, win, mu, w0, w2p, a0, a2p, g2, kkw, ka, rk, ones_bd, poolw, pscale):
    tm = min(ROW_TILE, seq)
    nt = seq // tm
    n = bsz * seq
    row_spec = lambda w: pl.BlockSpec((tm, w), lambda b, t: (b * nt + t, 0))
    consts = (win, mu, w0, w2p, a0, a2p, g2, kkw, ka, rk, ones_bd, poolw, pscale)
    return pl.pallas_call(
        _ev_front_kernel,
        grid=(bsz, nt),
        in_specs=[row_spec(D_MODEL)] + [_full(c.shape) for c in consts],
        out_specs=[row_spec(HALF)] * 9,
        out_shape=[jax.ShapeDtypeStruct((n, HALF), F32)] * 8 + [jax.ShapeDtypeStruct((n, HALF), BF16)],
        scratch_shapes=[pltpu.VMEM((8, RWKV_PROJ), F32), pltpu.VMEM((POOL_HALO, HALF), F32)],
        compiler_params=_params(("arbitrary", "arbitrary")),
        name="ev_front",
    )(x2, *consts)


def _scan_masks(c):
    n = 2 * c
    i = np.arange(n)[:, None]
    j = np.arange(n)[None, :]
    same = (i // c) == (j // c)
    masks = [same & (i > j), same & (i >= j), (i == j), (i // 8 == j // 8) & (i > j)]
    b = 8
    while b < c:
        masks.append((i // (2 * b) == j // (2 * b)) & (i // b != j // b) & (i > j))
        b *= 2
    return np.stack(masks).astype(np.float32)


def _rwkv_scan_kernel(r_ref, lw_ref, k_ref, v_ref, kk_ref, a_ref, tri_ref, masks_ref, y_o, h_s):
    c = SCAN_CHUNK
    rows = r_ref.shape[0]
    n_chunks = rows // c
    n = 2 * c

    @pl.when(pl.program_id(1) == 0)
    def _():
        h_s[...] = jnp.zeros_like(h_s)

    lw = lw_ref[...]
    cl = _split_dot_lhs(tri_ref[...], lw)
    cl_last_rows = [cl[(g + 1) * c - 1:(g + 1) * c, :] for g in range(n_chunks)]
    cl_last = jnp.concatenate([jnp.broadcast_to(m, (c, HALF)) for m in cl_last_rows], axis=0)
    e_in = jnp.exp(cl)
    e_ex = jnp.exp(cl - lw)
    e_neg = jnp.exp(-cl)
    e_rem = jnp.exp(cl_last - cl)

    kk = kk_ref[...]
    kv = k_ref[...]
    beta = kk * a_ref[...]
    abar = -kk * e_ex
    rbar = r_ref[...] * e_in
    btil = beta * e_neg
    ktil = kv * e_neg
    bhat = beta * e_rem
    khat = kv * e_rem
    vv = v_ref[...]

    m_strict = masks_ref[0]
    m_incl = masks_ref[1]
    eye = masks_ref[2]
    m_blk = masks_ref[3]
    n_merge = masks_ref.shape[0] - 4

    lane = lax.broadcasted_iota(jnp.int32, (c, PAIR), 1)
    even = lane < HEAD_DIM

    def stack_bd(m):
        return jnp.concatenate([jnp.where(even, m, 0.0), jnp.where(even, 0.0, m)], axis=0)

    def stack_2(m):
        return jnp.concatenate([m, m], axis=0)

    items = [(g, pr) for g in range(n_chunks) for pr in range(N_PAIRS)]

    def cut(m, it):
        g, pr = it
        return m[g * c:(g + 1) * c, pr * PAIR:(pr + 1) * PAIR]

    abar_bd = [stack_bd(cut(abar, it)) for it in items]
    rbar_bd = [stack_bd(cut(rbar, it)) for it in items]
    v_bd16 = [_bf(stack_bd(cut(vv, it))) for it in items]
    bhat_bd16 = [_bf(stack_bd(cut(bhat, it))) for it in items]
    khat_bd16 = [_bf(stack_bd(cut(khat, it))) for it in items]
    sc = [_dot_nt(_bf(jnp.concatenate([abar_bd[i], rbar_bd[i]], axis=0)),
                  _bf(jnp.concatenate([stack_2(cut(btil, it)), stack_2(cut(ktil, it))], axis=0)))
          for i, it in enumerate(items)]
    a_ab = [m[:n, :n] * m_strict for m in sc]
    a_ak16 = [_bf(m[:n, n:] * m_strict) for m in sc]
    a_rb16 = [_bf(m[n:, :n] * m_incl) for m in sc]
    a_rk16 = [_bf(m[n:, n:] * m_incl) for m in sc]

    a_d = [m * m_blk for m in a_ab]
    pw = [_dotb(m, m) for m in a_d]
    tinv = [_dotb(eye + a_d[i], eye + pw[i]) for i in range(len(items))]
    pw = [_dotb(m, m) for m in pw]
    tinv = [_dotb(tinv[i], eye + pw[i]) for i in range(len(items))]
    for lvl in range(n_merge):
        m_off = masks_ref[4 + lvl]
        tinv16 = [_bf(m) for m in tinv]
        at = [_dot(_bf(a_ab[i] * m_off), tinv16[i]) for i in range(len(items))]
        tinv = [tinv[i] + _dot(tinv16[i], _bf(at[i])) for i in range(len(items))]

    akv = [_dot(a_ak16[i], v_bd16[i]) for i in range(len(items))]
    wu16 = [_bf(_dotb(tinv[i], jnp.concatenate([abar_bd[i], akv[i]], axis=1)))
            for i in range(len(items))]
    arb_wu = [_dot(a_rb16[i], wu16[i]) for i in range(len(items))]
    qhat16 = [_bf(rbar_bd[i] + arb_wu[i][:, :PAIR]) for i in range(len(items))]
    y0 = [arb_wu[i][:, PAIR:] + _dot(a_rk16[i], v_bd16[i]) for i in range(len(items))]
    mg = [_dot_tn(bhat_bd16[i], wu16[i]) for i in range(len(items))]
    m_mat16 = [_bf(m[:, :PAIR]) for m in mg]
    g_mat = [mg[i][:, PAIR:] + _dot_tn(khat_bd16[i], v_bd16[i]) for i in range(len(items))]

    h = [h_s[pr] for pr in range(N_PAIRS)]
    for i, (g, pr) in enumerate(items):
        h16 = _bf(h[pr])
        y_bd = _dot(qhat16[i], h16) + y0[i]
        y_o[g * c:(g + 1) * c, pr * PAIR:(pr + 1) * PAIR] = y_bd[:c, :] + y_bd[c:, :]
        p_last = jnp.exp(cl_last_rows[g][:, pr * PAIR:(pr + 1) * PAIR])
        p_col = jnp.sum(eye * p_last, axis=1, keepdims=True)
        h[pr] = p_col * h[pr] + _dot(m_mat16[i], h16) + g_mat[i]
    for pr in range(N_PAIRS):
        h_s[pr] = h[pr]


def _split_dot_lhs(a_bf16, b):
    hi = _bf(b)
    lo = _bf(b - hi.astype(F32))
    return _dot(a_bf16, hi) + _dot(a_bf16, lo)


def _rwkv_scan(r, lw, k, v, kk, a, bsz, seq):
    c = SCAN_CHUNK
    rows = min(SCAN_ROWS, seq)
    nb = seq // rows
    n = bsz * seq
    tri = jnp.asarray(np.kron(np.eye(rows // c), np.tril(np.ones((c, c)))), BF16)
    masks = jnp.asarray(_scan_masks(c))
    row_spec = pl.BlockSpec((rows, HALF), lambda b, t: (b * nb + t, 0))
    return pl.pallas_call(
        _rwkv_scan_kernel,
        grid=(bsz, nb),
        in_specs=[row_spec] * 6 + [_full(tri.shape), _full(masks.shape)],
        out_specs=row_spec,
        out_shape=jax.ShapeDtypeStruct((n, HALF), F32),
        scratch_shapes=[pltpu.VMEM((N_PAIRS, PAIR, PAIR), F32)],
        compiler_params=_params(("arbitrary", "arbitrary")),
        name="rwkv_scan",
    )(r, lw, k, v, kk, a, tri, masks)


def _ev_back_kernel(y_ref, g_ref, bonus_ref, ypool_ref, x_ref, ones_ref, lng_ref, lnb_ref,
                    wout_ref, mg_ref, mb_ref, o_ref):
    hn = _head_norm(y_ref[...], ones_ref[...], RWKV_NORM_EPS)
    y_rwkv = (hn * lng_ref[...] + lnb_ref[...] + bonus_ref[...]) * g_ref[...]
    h = _dot(_bf(y_rwkv), wout_ref[0:HALF, :]) + _dot(ypool_ref[...], wout_ref[HALF:, :])
    o_ref[...] = _layer_norm(ALPHA * x_ref[...] + h, mg_ref[...], mb_ref[...], LN_EPS)


def _ev_back(y, g, bonus, ypool, x2, ones_bd, lng, lnb, wout, mg, mb):
    n = x2.shape[0]
    tm = min(ROW_TILE, n)
    half_spec = pl.BlockSpec((tm, HALF), lambda i: (i, 0))
    full_spec = pl.BlockSpec((tm, D_MODEL), lambda i: (i, 0))
    consts = (ones_bd, lng, lnb, wout, mg, mb)
    return pl.pallas_call(
        _ev_back_kernel,
        grid=(n // tm,),
        in_specs=[half_spec] * 4 + [full_spec] + [_full(c.shape) for c in consts],
        out_specs=full_spec,
        out_shape=jax.ShapeDtypeStruct((n, D_MODEL), F32),
        compiler_params=_params(("parallel",)),
        name="ev_back",
    )(y, g, bonus, ypool, x2, *consts)


MOE_TILE = 512
MOE_BLOCK = 512
SEG_ALIGN = 16
MOE_TILE_PAD = 640
GROUP_WIDTH = EXPERTS_PER_GROUP * D_EXPERT


def _pad_to(v, m):
    return ((v + (m - 1)) // m) * m


def _seg_layout(counts):
    padded = [_pad_to(c, SEG_ALIGN) for c in counts]
    starts = [0]
    for g in range(1, MOE_GROUPS):
        starts.append(starts[-1] + padded[g - 1])
    return padded, starts


def _moe_route_kernel(x_ref, rw2_ref, rb_ref, tri_ref,
                      xs_hbm, gs_hbm, pos_o, meta_o,
                      xbuf_s, gbuf_s, zx_s, zg_s, cnt_s, pend_s, sem):
    t = pl.program_id(0)
    nt = pl.num_programs(0)
    tm = x_ref.shape[0]
    cap = xs_hbm.shape[0] // MOE_GROUPS
    slot = t % 2

    def seg_copies(sl, src_row, dst_row):
        src_row = pl.multiple_of(src_row, SEG_ALIGN)
        dst_row = pl.multiple_of(dst_row, SEG_ALIGN)
        return (pltpu.make_async_copy(xbuf_s.at[sl, pl.ds(src_row, SEG_ALIGN)],
                                      xs_hbm.at[pl.ds(dst_row, SEG_ALIGN)], sem.at[sl]),
                pltpu.make_async_copy(gbuf_s.at[sl, pl.ds(src_row, SEG_ALIGN)],
                                      gs_hbm.at[pl.ds(dst_row, SEG_ALIGN)], sem.at[sl]))

    def wait_slot(sl):
        def body(i, carry):
            for cp in seg_copies(sl, 0, 0):
                cp.wait()
            return carry
        lax.fori_loop(0, pend_s[sl], body, 0)
        pend_s[sl] = 0

    @pl.when(t == 0)
    def _():
        for g in range(MOE_GROUPS):
            cnt_s[g] = 0
        pend_s[0] = 0
        pend_s[1] = 0
        zx_s[...] = jnp.zeros_like(zx_s)
        zg_s[...] = jnp.zeros_like(zg_s)

    x = x_ref[...]
    xh = _bf(x)
    xl = _bf(x - xh.astype(F32))
    both = _dot(xh, rw2_ref[...])
    logits = (both[:, :LANES] + both[:, LANES:] + _dot(xl, rw2_ref[:, :LANES])
              + rb_ref[...])
    lane = lax.broadcasted_iota(jnp.int32, logits.shape, 1)
    lanef = lane.astype(F32)
    is_g = (lane >= N_EXPERTS) & (lane < N_EXPERTS + MOE_GROUPS)
    gl = jnp.where(is_g, logits, NEG_BIG)
    gmax = jnp.max(gl, axis=1, keepdims=True)
    gidx = jnp.min(jnp.where(gl == gmax, lanef, 1e9), axis=1, keepdims=True) - float(N_EXPERTS)
    gden = jnp.sum(jnp.where(is_g, jnp.exp(gl - gmax), 0.0), axis=1, keepdims=True)
    g_w = 1.0 / gden
    grp = jnp.floor(lanef * (1.0 / EXPERTS_PER_GROUP))
    sel = (lane < N_EXPERTS) & (grp == gidx)
    el = jnp.where(sel, logits, NEG_BIG)
    m1 = jnp.max(el, axis=1, keepdims=True)
    i1 = jnp.min(jnp.where(el == m1, lanef, 1e9), axis=1, keepdims=True)
    el2 = jnp.where(lanef == i1, NEG_BIG, el)
    m2 = jnp.max(el2, axis=1, keepdims=True)
    i2 = jnp.min(jnp.where(el2 == m2, lanef, 1e9), axis=1, keepdims=True)
    e21 = jnp.exp(m2 - m1)
    w_top = g_w / (1.0 + e21)
    w_sec = g_w * e21 / (1.0 + e21)
    base = gidx * float(EXPERTS_PER_GROUP)
    gates8 = jnp.where(lanef == i1 - base, w_top, jnp.where(lanef == i2 - base, w_sec, 0.0))

    onehot = jnp.where(lanef == gidx, 1.0, 0.0)
    counts = [jnp.sum(onehot[:, g:g + 1]).astype(jnp.int32) for g in range(MOE_GROUPS)]
    padded, starts = _seg_layout(counts)
    rank = _dot(tri_ref[...], _bf(onehot))
    counts_row = jnp.sum(onehot, axis=0, keepdims=True)
    padded_row = jnp.floor((counts_row + float(SEG_ALIGN - 1)) * (1.0 / SEG_ALIGN)) * float(SEG_ALIGN)
    padded8 = jnp.broadcast_to(padded_row, (SUBLANES, LANES))
    lane8 = lax.broadcasted_iota(jnp.int32, (SUBLANES, LANES), 1)
    start8 = jnp.zeros((SUBLANES, LANES), F32)
    for k in range(1, MOE_GROUPS):
        start8 = start8 + jnp.where(lane8 >= k, pltpu.roll(padded8, k, 1), 0.0)
    pos = jnp.sum(onehot * (rank + start8[0:1, :]), axis=1, keepdims=True)
    pos_o[...] = pos
    perm_t = jnp.where(pos == lax.broadcasted_iota(jnp.int32, (tm, MOE_TILE_PAD), 1).astype(F32),
                       1.0, 0.0).astype(BF16)

    wait_slot(slot)
    g_hi = _bf(gates8)
    g_lo = _bf(gates8 - g_hi.astype(F32))
    srt = _dot_tn(perm_t, jnp.concatenate([xh, g_hi, g_lo], axis=1))
    xbuf_s[slot] = srt[:, :D_MODEL].astype(BF16)
    gbuf_s[slot] = srt[:, D_MODEL:D_MODEL + LANES] + srt[:, D_MODEL + LANES:]

    n_issued = 0
    for g in range(MOE_GROUPS):
        dst0 = g * cap + cnt_s[g]
        meta_o[t * 2 * MOE_GROUPS + g] = dst0
        meta_o[t * 2 * MOE_GROUPS + MOE_GROUPS + g] = padded[g]
        nch = padded[g] // SEG_ALIGN

        def issue(i, carry, g=g, dst0=dst0):
            for cp in seg_copies(slot, starts[g] + i * SEG_ALIGN, dst0 + i * SEG_ALIGN):
                cp.start()
            return carry
        lax.fori_loop(0, nch, issue, 0)
        cnt_s[g] = cnt_s[g] + padded[g]
        n_issued = n_issued + nch
    pend_s[slot] = n_issued

    @pl.when(t == nt - 1)
    def _():
        wait_slot(0)
        wait_slot(1)
        for g in range(MOE_GROUPS):
            dst0 = g * cap + cnt_s[g]

            def zero_copies(i, dst0=dst0):
                row = pl.multiple_of(dst0 + i * SEG_ALIGN, SEG_ALIGN)
                return (pltpu.make_async_copy(zx_s, xs_hbm.at[pl.ds(row, SEG_ALIGN)], sem.at[0]),
                        pltpu.make_async_copy(zg_s, gs_hbm.at[pl.ds(row, SEG_ALIGN)], sem.at[0]))

            def zissue(i, carry):
                for cp in zero_copies(i):
                    cp.start()
                return carry

            def zwait(i, carry):
                for cp in zero_copies(i):
                    cp.wait()
                return carry
            lax.fori_loop(0, MOE_BLOCK // SEG_ALIGN, zissue, 0)
            lax.fori_loop(0, MOE_BLOCK // SEG_ALIGN, zwait, 0)


def _moe_route(x2, rw2, rb):
    n = x2.shape[0]
    tm = MOE_TILE
    nt = n // tm
    cap = _pad_to(n + nt * SEG_ALIGN + MOE_BLOCK, MOE_BLOCK)
    tri = jnp.asarray(np.tril(np.ones((tm, tm)), -1), BF16)
    any_spec = pl.BlockSpec(memory_space=pl.ANY)
    return pl.pallas_call(
        _moe_route_kernel,
        grid=(nt,),
        in_specs=[pl.BlockSpec((tm, D_MODEL), lambda i: (i, 0)), _full(rw2.shape),
                  _full(rb.shape), _full(tri.shape)],
        out_specs=[any_spec, any_spec, pl.BlockSpec((tm, 1), lambda i: (i, 0)),
                   pl.BlockSpec(memory_space=pltpu.SMEM)],
        out_shape=[jax.ShapeDtypeStruct((MOE_GROUPS * cap, D_MODEL), BF16),
                   jax.ShapeDtypeStruct((MOE_GROUPS * cap, LANES), F32),
                   jax.ShapeDtypeStruct((n, 1), F32),
                   jax.ShapeDtypeStruct((nt * 2 * MOE_GROUPS,), jnp.int32)],
        scratch_shapes=[pltpu.VMEM((2, MOE_TILE_PAD, D_MODEL), BF16), pltpu.VMEM((2, MOE_TILE_PAD, LANES), F32),
                        pltpu.VMEM((SEG_ALIGN, D_MODEL), BF16), pltpu.VMEM((SEG_ALIGN, LANES), F32),
                        pltpu.SMEM((MOE_GROUPS,), jnp.int32), pltpu.SMEM((2,), jnp.int32),
                        pltpu.SemaphoreType.DMA((2,))],
        compiler_params=_params(("arbitrary",)),
        name="moe_route",
    )(x2, rw2, rb, tri)


def _moe_experts_kernel(brow_ref, bgrp_ref, nval_ref, xs_ref, gs_ref, w1_ref, w3_ref, w2_ref, ex_ref, ys_ref,
                        w1_s, w3_s, w2_s):
    b = pl.program_id(0)
    valid = b < nval_ref[0]
    new_group = bgrp_ref[b] != bgrp_ref[jnp.maximum(b - 1, 0)]

    @pl.when(valid & ((b == 0) | new_group))
    def _():
        for e in range(EXPERTS_PER_GROUP):
            w1_s[:, e * D_EXPERT:(e + 1) * D_EXPERT] = _bf(w1_ref[e])
            w3_s[:, e * D_EXPERT:(e + 1) * D_EXPERT] = _bf(w3_ref[e])
            w2_s[e * D_EXPERT:(e + 1) * D_EXPERT, :] = _bf(w2_ref[e])

    @pl.when(valid)
    def _():
        xb = xs_ref[...]
        h1 = _dot(xb, w1_s[...])
        h3 = _dot(xb, w3_s[...])
        gexp = _split_dot(gs_ref[...], ex_ref[...])
        act = h1 * _sigmoid(h1) * h3 * gexp
        ys_ref[...] = _bf(_dot(_bf(act), w2_s[...]))

    @pl.when(jnp.logical_not(valid))
    def _():
        ys_ref[...] = jnp.zeros_like(ys_ref)


def _moe_experts(xs, gs, brow, bgrp, nval, w1g, w3g, w2g, expand8):
    rows = xs.shape[0]
    nb = brow.shape[0]
    rb = MOE_BLOCK
    trash = rows // rb

    def out_map(b, brow_ref, bgrp_ref, nval_ref):
        return (jnp.where(b < nval_ref[0], brow_ref[b], trash), 0)

    grid_spec = pltpu.PrefetchScalarGridSpec(
        num_scalar_prefetch=3,
        grid=(nb,),
        in_specs=[pl.BlockSpec((rb, D_MODEL), lambda b, br, bg, nv: (br[b], 0)),
                  pl.BlockSpec((rb, LANES), lambda b, br, bg, nv: (br[b], 0)),
                  pl.BlockSpec((EXPERTS_PER_GROUP, D_MODEL, D_EXPERT), lambda b, br, bg, nv: (bg[b], 0, 0)),
                  pl.BlockSpec((EXPERTS_PER_GROUP, D_MODEL, D_EXPERT), lambda b, br, bg, nv: (bg[b], 0, 0)),
                  pl.BlockSpec((EXPERTS_PER_GROUP, D_EXPERT, D_MODEL), lambda b, br, bg, nv: (bg[b], 0, 0)),
                  pl.BlockSpec(expand8.shape, lambda b, br, bg, nv: (0, 0))],
        out_specs=pl.BlockSpec((rb, D_MODEL), out_map),
        scratch_shapes=[pltpu.VMEM((D_MODEL, GROUP_WIDTH), BF16), pltpu.VMEM((D_MODEL, GROUP_WIDTH), BF16),
                        pltpu.VMEM((GROUP_WIDTH, D_MODEL), BF16)],
    )
    return pl.pallas_call(
        _moe_experts_kernel,
        grid_spec=grid_spec,
        out_shape=jax.ShapeDtypeStruct((rows + rb, D_MODEL), BF16),
        compiler_params=_params(("arbitrary",)),
        name="moe_experts",
    )(brow, bgrp, nval, xs, gs, w1g, w3g, w2g, expand8)


def _moe_combine_kernel(meta_ref, ys_hbm, x_ref, pos_ref, fg_ref, fb_ref, o_ref, ybuf_s, sem):
    t = pl.program_id(0)
    nt = pl.num_programs(0)
    tm = x_ref.shape[0]
    slot = t % 2

    def tile_segments(tt):
        dst = [meta_ref[tt * 2 * MOE_GROUPS + g] for g in range(MOE_GROUPS)]
        padded = [meta_ref[tt * 2 * MOE_GROUPS + MOE_GROUPS + g] for g in range(MOE_GROUPS)]
        starts = [0]
        for g in range(1, MOE_GROUPS):
            starts.append(starts[-1] + padded[g - 1])
        return dst, padded, starts

    def seg_copy(sl, src_row, dst_row):
        src_row = pl.multiple_of(src_row, SEG_ALIGN)
        dst_row = pl.multiple_of(dst_row, SEG_ALIGN)
        return pltpu.make_async_copy(ys_hbm.at[pl.ds(src_row, SEG_ALIGN)],
                                     ybuf_s.at[sl, pl.ds(dst_row, SEG_ALIGN)], sem.at[sl])

    def fetch(tt, sl):
        dst, padded, starts = tile_segments(tt)
        for g in range(MOE_GROUPS):
            def body(i, carry, g=g):
                seg_copy(sl, dst[g] + i * SEG_ALIGN, starts[g] + i * SEG_ALIGN).start()
                return carry
            lax.fori_loop(0, padded[g] // SEG_ALIGN, body, 0)

    def wait_tile(tt, sl):
        _, padded, _ = tile_segments(tt)
        total = padded[0] + padded[1] + padded[2] + padded[3]

        def body(i, carry):
            seg_copy(sl, 0, 0).wait()
            return carry
        lax.fori_loop(0, total // SEG_ALIGN, body, 0)

    @pl.when(t == 0)
    def _():
        ybuf_s[...] = jnp.zeros_like(ybuf_s)
        fetch(0, 0)

    @pl.when(t + 1 < nt)
    def _():
        fetch(t + 1, 1 - slot)

    wait_tile(t, slot)
    perm_t = jnp.where(pos_ref[...] == lax.broadcasted_iota(jnp.int32, (tm, MOE_TILE_PAD), 1).astype(F32),
                       1.0, 0.0).astype(BF16)
    f = _dot(perm_t, ybuf_s[slot])
    o_ref[...] = _layer_norm(ALPHA * x_ref[...] + f, fg_ref[...], fb_ref[...], LN_EPS)


def _moe_combine(ys, x2, pos, meta, fg, fb):
    n = x2.shape[0]
    tm = MOE_TILE
    grid_spec = pltpu.PrefetchScalarGridSpec(
        num_scalar_prefetch=1,
        grid=(n // tm,),
        in_specs=[pl.BlockSpec(memory_space=pl.ANY),
                  pl.BlockSpec((tm, D_MODEL), lambda i, m: (i, 0)),
                  pl.BlockSpec((tm, 1), lambda i, m: (i, 0)),
                  pl.BlockSpec(fg.shape, lambda i, m: (0, 0)),
                  pl.BlockSpec(fb.shape, lambda i, m: (0, 0))],
        out_specs=pl.BlockSpec((tm, D_MODEL), lambda i, m: (i, 0)),
        scratch_shapes=[pltpu.VMEM((2, MOE_TILE_PAD, D_MODEL), BF16), pltpu.SemaphoreType.DMA((2,))],
    )
    return pl.pallas_call(
        _moe_combine_kernel,
        grid_spec=grid_spec,
        out_shape=jax.ShapeDtypeStruct((n, D_MODEL), F32),
        compiler_params=_params(("arbitrary",)),
        name="moe_combine",
    )(meta, ys, x2, pos, fg, fb)


def _moe(x2, rw2, rb, w1g, w3g, w2g, expand8, fg, fb):
    n = x2.shape[0]
    xs, gs, pos, meta = _moe_route(x2, rw2, rb)
    cap = xs.shape[0] // MOE_GROUPS
    last = meta[-2 * MOE_GROUPS:]
    group_rows = last[:MOE_GROUPS] + last[MOE_GROUPS:] - jnp.arange(MOE_GROUPS, dtype=jnp.int32) * cap
    nblk = (group_rows + (MOE_BLOCK - 1)) // MOE_BLOCK
    ends = jnp.cumsum(nblk)
    max_rows = n + (n // MOE_TILE) * MOE_GROUPS * (SEG_ALIGN - 1)
    nb_max = max_rows // MOE_BLOCK + MOE_GROUPS + 1
    b = jnp.arange(nb_max, dtype=jnp.int32)
    bgrp = jnp.minimum(jnp.sum((b[:, None] >= ends[None, :]).astype(jnp.int32), axis=1), MOE_GROUPS - 1)
    brow = bgrp * (cap // MOE_BLOCK) + (b - (ends - nblk)[bgrp])
    nval = ends[-1:]
    last_valid = jnp.maximum(nval[0] - 1, 0)
    brow = jnp.where(b < nval[0], brow, brow[last_valid]).astype(jnp.int32)
    bgrp = jnp.where(b < nval[0], bgrp, bgrp[last_valid]).astype(jnp.int32)
    ys = _moe_experts(xs, gs, brow, bgrp, nval.astype(jnp.int32), w1g, w3g, w2g, expand8)
    return _moe_combine(ys, x2, pos, meta, fg, fb)


def _rope_table_kernel(cos_o, sin_o):
    tm = cos_o.shape[0]
    half = HEAD_DIM // 2
    pos = (pl.program_id(0) * tm + lax.broadcasted_iota(jnp.int32, (tm, LANES), 0)).astype(F32)
    lane = lax.broadcasted_iota(jnp.int32, (tm, LANES), 1)
    idx = (lane & (half - 1)).astype(F32)
    inv = jnp.exp(idx * (-math.log(ROPE_BASE) / half))
    ang = pos * inv
    first = (lane & half) == 0
    c = jnp.cos(ang)
    s = jnp.sin(ang)
    s = jnp.where(first, -s, s)
    for q in range(HALF // LANES):
        cos_o[:, q * LANES:(q + 1) * LANES] = c
        sin_o[:, q * LANES:(q + 1) * LANES] = s


def _rope_table(seq):
    tm = min(ROW_TILE, seq)
    spec = pl.BlockSpec((tm, HALF), lambda i: (i, 0))
    return pl.pallas_call(
        _rope_table_kernel,
        grid=(seq // tm,),
        in_specs=[],
        out_specs=[spec, spec],
        out_shape=[jax.ShapeDtypeStruct((seq, HALF), F32)] * 2,
        compiler_params=_params(("parallel",)),
        name="rope_table",
    )()


def _od_front_kernel(x_ref, win_ref, cw_ref, cb_ref, clg_ref, clb_ref, cos_ref, sin_ref,
                     yconv_o, q_o, k_o, v_o, sg_o, ubuf_s):
    t = pl.program_id(1)
    tm = x_ref.shape[0]

    @pl.when(t == 0)
    def _():
        ubuf_s[0:CONV_HALO, :] = jnp.zeros((CONV_HALO, HALF), F32)

    p = _dot(_bf(x_ref[...]), win_ref[...])
    ca = p[:, 0:HALF]
    cb = p[:, HALF:2 * HALF]
    q = p[:, 2 * HALF:3 * HALF]
    k = p[:, 3 * HALF:4 * HALF]
    v = p[:, 4 * HALF:5 * HALF]
    gr = p[:, 5 * HALF:6 * HALF]

    ubuf_s[CONV_HALO:, :] = ca * _sigmoid(cb)
    acc = jnp.zeros((tm, HALF), F32) + cb_ref[...]
    for rho in range(SUBLANES):
        ext = 0 if rho == 0 else SUBLANES
        part = None
        for j in range(CONV_WIDTH):
            off = CONV_HALO - (CONV_WIDTH - 1) + j
            if off % SUBLANES != rho:
                continue
            base = off - rho
            term = cw_ref[j:j + 1, :] * ubuf_s[base:base + tm + ext, :]
            part = term if part is None else part + term
        if rho == 0:
            acc = acc + part
        else:
            acc = acc + pltpu.roll(part, tm + ext - rho, 0)[0:tm, :]
    ubuf_s[0:CONV_HALO, :] = ubuf_s[tm:tm + CONV_HALO, :]
    ln = _layer_norm(acc, clg_ref[...], clb_ref[...], LN_EPS)
    yconv_o[...] = _bf(ln * _sigmoid(ln))

    lane = lax.broadcasted_iota(jnp.int32, (tm, HALF), 1)
    first = (lane & (HEAD_DIM // 2)) == 0
    cos = cos_ref[...]
    sin = sin_ref[...]

    def rot(m):
        partner = jnp.where(first, pltpu.roll(m, HALF - HEAD_DIM // 2, 1), pltpu.roll(m, HEAD_DIM // 2, 1))
        return m * cos + partner * sin

    q_o[...] = _bf(rot(q))
    k_o[...] = _bf(rot(k) * (HEAD_DIM ** -0.5))
    v_o[...] = _bf(v)
    sg_o[...] = gr * _sigmoid(gr)


def _od_front(x2, bsz, seq, win, cw, cb, clg, clb, cos_t, sin_t):
    tm = min(ROW_TILE, seq)
    nt = seq // tm
    n = bsz * seq
    row_spec = lambda w: pl.BlockSpec((tm, w), lambda b, t: (b * nt + t, 0))
    tab_spec = pl.BlockSpec((tm, HALF), lambda b, t: (t, 0))
    consts = (win, cw, cb, clg, clb)
    sds = lambda dt: jax.ShapeDtypeStruct((n, HALF), dt)
    return pl.pallas_call(
        _od_front_kernel,
        grid=(bsz, nt),
        in_specs=[row_spec(D_MODEL)] + [_full(c.shape) for c in consts] + [tab_spec, tab_spec],
        out_specs=[row_spec(HALF)] * 5,
        out_shape=[sds(BF16), sds(BF16), sds(BF16), sds(BF16), sds(F32)],
        scratch_shapes=[pltpu.VMEM((tm + CONV_HALO, HALF), F32)],
        compiler_params=_params(("arbitrary", "arbitrary")),
        name="od_front",
    )(x2, *consts, cos_t, sin_t)


def _ret_consts(c):
    h = np.arange(N_HEADS, dtype=np.float64)
    log_gamma = np.log1p(-np.power(2.0, -5.0 - h))
    idx = np.arange(c, dtype=np.float64)
    diff = idx[:, None] - idx[None, :]
    dmask = np.where(diff >= 0, np.exp(np.maximum(diff, 0.0)[None] * log_gamma[:, None, None]), 0.0)
    xi = np.exp((idx + 1.0)[:, None] * log_gamma[None, :])
    zeta = np.exp((c - 1.0 - idx)[:, None] * log_gamma[None, :])
    xi = np.repeat(xi, HEAD_DIM, axis=1)
    zeta = np.repeat(zeta, HEAD_DIM, axis=1)
    gamma_c = np.exp(c * log_gamma)
    lane_head = np.arange(PAIR) // HEAD_DIM
    gdiag = np.zeros((N_PAIRS, PAIR, PAIR))
    for pr in range(N_PAIRS):
        same = lane_head[:, None] == lane_head[None, :]
        gdiag[pr] = np.where(same, gamma_c[2 * pr + lane_head][:, None], 0.0)
    bd = (lane_head[:, None] == lane_head[None, :]).astype(np.float32)
    return (dmask.astype(np.float32), xi.astype(np.float32), zeta.astype(np.float32),
            gdiag.astype(np.float32), bd)


def _retention_kernel(q_ref, k_ref, v_ref, sg_ref, dmask_ref, xi_ref, zeta_ref, gdiag_ref, bd_ref,
                      ones_ref, gng_ref, gnb_ref, o_ref, r_s):
    c = q_ref.shape[0]

    @pl.when(pl.program_id(1) == 0)
    def _():
        r_s[...] = jnp.zeros_like(r_s)

    lane = lax.broadcasted_iota(jnp.int32, (c, PAIR), 1)
    even = lane < HEAD_DIM
    bd = bd_ref[...]
    pairs = range(N_PAIRS)
    sls = [slice(pr * PAIR, (pr + 1) * PAIR) for pr in pairs]
    qp = [q_ref[:, sl] for sl in sls]
    kp = [k_ref[:, sl] for sl in sls]
    vp = [v_ref[:, sl] for sl in sls]
    zero = jnp.zeros_like(qp[0])
    s_even = [_bf(_dot_nt(jnp.where(even, qp[pr], zero), kp[pr]) * dmask_ref[2 * pr]) for pr in pairs]
    s_odd = [_bf(_dot_nt(jnp.where(even, zero, qp[pr]), kp[pr]) * dmask_ref[2 * pr + 1]) for pr in pairs]
    r0 = [r_s[pr] for pr in pairs]
    cross = [_dot(_bf(qp[pr].astype(F32) * xi_ref[:, sls[pr]]), _bf(r0[pr])) for pr in pairs]
    kz = [_bf(kp[pr].astype(F32) * zeta_ref[:, sls[pr]]) for pr in pairs]
    kv = [_dot_tn(kz[pr], vp[pr]) for pr in pairs]
    intra = [jnp.where(even, _dot(s_even[pr], vp[pr]), _dot(s_odd[pr], vp[pr])) for pr in pairs]
    ones_pair = ones_ref[...]
    for pr in pairs:
        r_s[pr] = gdiag_ref[pr] * r0[pr] + bd * kv[pr]
        ret = _head_norm(intra[pr] + cross[pr], ones_pair, LN_EPS)
        o_ref[:, sls[pr]] = _bf(sg_ref[:, sls[pr]] * (ret * gng_ref[:, sls[pr]] + gnb_ref[:, sls[pr]]))


def _retention(q, k, v, sg, bsz, seq, gng, gnb):
    c = min(RET_CHUNK, seq)
    nc = seq // c
    n = bsz * seq
    dmask, xi, zeta, gdiag, bd = (jnp.asarray(m) for m in _ret_consts(c))
    ones_pair = jnp.asarray(np.kron(np.eye(2), np.ones((HEAD_DIM, HEAD_DIM))), BF16)
    row_spec = pl.BlockSpec((c, HALF), lambda b, t: (b * nc + t, 0))
    consts = (dmask, xi, zeta, gdiag, bd, ones_pair, gng, gnb)
    return pl.pallas_call(
        _retention_kernel,
        grid=(bsz, nc),
        in_specs=[row_spec] * 4 + [_full(m.shape) for m in consts],
        out_specs=row_spec,
        out_shape=jax.ShapeDtypeStruct((n, HALF), BF16),
        scratch_shapes=[pltpu.VMEM((N_PAIRS, PAIR, PAIR), F32)],
        compiler_params=_params(("arbitrary", "arbitrary")),
        name="retention",
    )(q, k, v, sg, *consts)


def _od_back_kernel(ya_ref, yb_ref, x_ref, wout_ref, mg_ref, mb_ref, o_ref):
    h = _dot(ya_ref[...], wout_ref[0:HALF, :]) + _dot(yb_ref[...], wout_ref[HALF:, :])
    o_ref[...] = _layer_norm(ALPHA * x_ref[...] + h, mg_ref[...], mb_ref[...], LN_EPS)


def _od_back(ya, yb, x2, wout, mg, mb):
    n = x2.shape[0]
    tm = min(ROW_TILE, n)
    half_spec = pl.BlockSpec((tm, HALF), lambda i: (i, 0))
    full_spec = pl.BlockSpec((tm, D_MODEL), lambda i: (i, 0))
    consts = (wout, mg, mb)
    return pl.pallas_call(
        _od_back_kernel,
        grid=(n // tm,),
        in_specs=[half_spec, half_spec, full_spec] + [_full(c.shape) for c in consts],
        out_specs=full_spec,
        out_shape=jax.ShapeDtypeStruct((n, D_MODEL), F32),
        compiler_params=_params(("parallel",)),
        name="od_back",
    )(ya, yb, x2, *consts)


def _row(v):
    return v.reshape(1, -1).astype(F32)


def _moe_weights(rg_w, rg_b, re_w, re_b, e_w1, e_w3, e_w2):
    pad = LANES - N_EXPERTS - MOE_GROUPS
    rw = jnp.concatenate([re_w, rg_w, jnp.zeros((D_MODEL, pad), F32)], axis=1)
    rwh = _bf(rw)
    rwl = _bf(rw - rwh.astype(F32))
    rb = jnp.concatenate([re_b, rg_b, jnp.zeros((pad,), F32)]).reshape(1, LANES)
    rw2 = jnp.concatenate([rwh, rwl], axis=1)
    return rw2, rb, e_w1, e_w3, e_w2


def kernel(x, ev_w_in, ev_mu, ev_w0, ev_w2, ev_a0, ev_a2, ev_g2, ev_k_k, ev_k_a, ev_r_k, ev_lnx_g, ev_lnx_b, ev_pool_w, ev_pool_scale, ev_w_out, od_w_in, od_conv_w, od_conv_b, od_cln_g, od_cln_b, od_gn_g, od_gn_b, od_w_out, ln_mix_g, ln_mix_b, rg_w, rg_b, re_w, re_b, e_w1, e_w3, e_w2, ln_ffn_g, ln_ffn_b):
    bsz, seq, _ = x.shape
    x2 = x.reshape(bsz * seq, D_MODEL)
    ones_bd = jnp.asarray(np.kron(np.eye(N_HEADS), np.ones((HEAD_DIM, HEAD_DIM))), BF16)
    expand = jnp.asarray(
        np.pad(np.kron(np.eye(EXPERTS_PER_GROUP), np.ones((1, D_EXPERT))),
               ((0, LANES - EXPERTS_PER_GROUP), (0, 0))), BF16)
    zeros_lora = jnp.zeros((LORA_W, HALF), F32)

    w2p = jnp.concatenate([ev_w2[0], zeros_lora], axis=0)
    a2p = jnp.concatenate([zeros_lora, ev_a2[0]], axis=0)
    r, lw, k, v, kk, a, g, bonus, ypool = _ev_front(
        x2, bsz, seq, _bf(ev_w_in[0]), _row(ev_mu[0]), _row(ev_w0[0]), _bf(w2p), _row(ev_a0[0]), _bf(a2p),
        _bf(ev_g2[0]), _row(ev_k_k[0]), _row(ev_k_a[0]), _row(ev_r_k[0]), ones_bd, _bf(ev_pool_w[0]),
        _row(ev_pool_scale[0]))
    y = _rwkv_scan(r, lw, k, v, kk, a, bsz, seq)
    x2 = _ev_back(y, g, bonus, ypool, x2, ones_bd, _row(ev_lnx_g[0]), _row(ev_lnx_b[0]), _bf(ev_w_out[0]),
                  _row(ln_mix_g[0]), _row(ln_mix_b[0]))
    x2 = _moe(x2, *_moe_weights(rg_w[0], rg_b[0], re_w[0], re_b[0], e_w1[0], e_w3[0], e_w2[0]), expand,
              _row(ln_ffn_g[0]), _row(ln_ffn_b[0]))

    cos_t, sin_t = _rope_table(seq)
    yconv, q, kr, vr, sg = _od_front(x2, bsz, seq, _bf(od_w_in[0]), od_conv_w[0], _row(od_conv_b[0]),
                                     _row(od_cln_g[0]), _row(od_cln_b[0]), cos_t, sin_t)
    yret = _retention(q, kr, vr, sg, bsz, seq, _row(od_gn_g[0]), _row(od_gn_b[0]))
    x2 = _od_back(yconv, yret, x2, _bf(od_w_out[0]), _row(ln_mix_g[1]), _row(ln_mix_b[1]))
    x2 = _moe(x2, *_moe_weights(rg_w[1], rg_b[1], re_w[1], re_b[1], e_w1[1], e_w3[1], e_w2[1]), expand,
              _row(ln_ffn_g[1]), _row(ln_ffn_b[1]))
    return x2.reshape(bsz, seq, D_MODEL)
```

```python
import functools
import math

import numpy as np
import jax
import jax.numpy as jnp
from jax import lax
from jax.experimental import pallas as pl
from jax.experimental.pallas import tpu as pltpu

F32 = jnp.float32
BF16 = jnp.bfloat16

D_MODEL = 1024
HALF = D_MODEL // 2
HEAD_DIM = 64
N_HEADS = HALF // HEAD_DIM
PAIR = 2 * HEAD_DIM
N_PAIRS = N_HEADS // 2
LORA_W = 64
LORA_A = 64
LORA_G = 128
RWKV_PROJ = 3 * HALF + LORA_W + LORA_A + LORA_G
EVEN_PROJ = RWKV_PROJ + HALF
ODD_PROJ = 6 * HALF
RWKV_NORM_EPS = 64e-5
LN_EPS = 1e-5
POOL_WINDOWS = (2, 4, 8, 16)
POOL_HALO = 16
CONV_WIDTH = 31
CONV_HALO = 32
ROPE_BASE = 10000.0
MOE_GROUPS = 4
EXPERTS_PER_GROUP = 8
N_EXPERTS = MOE_GROUPS * EXPERTS_PER_GROUP
D_EXPERT = 128
DEPTH = 2
ALPHA = (2.0 * DEPTH) ** 0.25
LANES = 128
SUBLANES = 8
NEG_BIG = -1e30

ROW_TILE = 512
SCAN_CHUNK = 64
SCAN_ROWS = 256
RET_CHUNK = 256
VMEM_LIMIT = 56 * 1024 * 1024


def _bf(v):
    return v.astype(BF16)


def _dot(a, b):
    return jnp.dot(a, b, preferred_element_type=F32)


def _dot_nt(a, b):
    return lax.dot_general(a, b, (((1,), (1,)), ((), ())), preferred_element_type=F32)


def _dot_tn(a, b):
    return lax.dot_general(a, b, (((0,), (0,)), ((), ())), preferred_element_type=F32)


def _dotb(a, b):
    return _dot(_bf(a), _bf(b))


def _split_dot(a, b_bf16):
    hi = _bf(a)
    lo = _bf(a - hi.astype(F32))
    return _dot(hi, b_bf16) + _dot(lo, b_bf16)


def _sigmoid(v):
    return 1.0 / (1.0 + jnp.exp(-v))


def _layer_norm(v, g, b, eps):
    mu = jnp.mean(v, axis=-1, keepdims=True)
    d = v - mu
    var = jnp.mean(d * d, axis=-1, keepdims=True)
    return d * lax.rsqrt(var + eps) * g + b


def _head_norm(v, ones_bd, eps):
    mu = _split_dot(v, ones_bd) * (1.0 / HEAD_DIM)
    d = v - mu
    var = _split_dot(d * d, ones_bd) * (1.0 / HEAD_DIM)
    return d * lax.rsqrt(var + eps)


def _full(shape):
    nd = len(shape)
    return pl.BlockSpec(shape, lambda *_: (0,) * nd)


def _params(sem):
    return pltpu.CompilerParams(dimension_semantics=sem, vmem_limit_bytes=VMEM_LIMIT)


def _ev_front_kernel(x_ref, win_ref, mu_ref, w0_ref, w2p_ref, a0_ref, a2p_ref, g2_ref, kkw_ref,
                     ka_ref, rk_ref, ones_ref, poolw_ref, pscale_ref,
                     r_o, lw_o, k_o, v_o, kk_o, a_o, g_o, bonus_o, ypool_o,
                     prow_s, ucarry_s):
    t = pl.program_id(1)
    tm = x_ref.shape[0]

    @pl.when(t == 0)
    def _():
        prow_s[...] = jnp.zeros_like(prow_s)
        ucarry_s[...] = jnp.zeros_like(ucarry_s)

    p = _dot(_bf(x_ref[...]), win_ref[...])
    pr = p[:, :RWKV_PROJ]
    u = p[:, RWKV_PROJ:]

    row = lax.broadcasted_iota(jnp.int32, (tm, RWKV_PROJ), 0)
    prev = jnp.where(row == 0, prow_s[0:1, :], pltpu.roll(pr, 1, 0))
    prow_s[0:1, :] = pr[tm - 1:tm, :]
    z = pr + mu_ref[...] * (prev - pr)

    r = z[:, 0:HALF]
    k = z[:, HALF:2 * HALF]
    v = z[:, 2 * HALF:3 * HALF]
    zl = z[:, 3 * HALF:3 * HALF + LORA_W + LORA_A]
    zg = z[:, 3 * HALF + LORA_W + LORA_A:RWKV_PROJ]

    yw = w0_ref[...] + _dotb(jnp.tanh(zl), w2p_ref[...])
    lw = -math.exp(-0.5) * _sigmoid(yw)
    a = _sigmoid(a0_ref[...] + _dotb(zl, a2p_ref[...]))
    g = _dotb(_sigmoid(zg), g2_ref[...])

    ones_bd = ones_ref[...]
    kk = k * kkw_ref[...]
    kk = kk * lax.rsqrt(jnp.maximum(_split_dot(kk * kk, ones_bd), 1e-24))
    kmod = k * (1.0 + (a - 1.0) * ka_ref[...])
    bonus = _split_dot(r * kmod * rk_ref[...], ones_bd) * v

    r_o[...] = r
    lw_o[...] = lw
    k_o[...] = kmod
    v_o[...] = v
    kk_o[...] = kk
    a_o[...] = a
    g_o[...] = g
    bonus_o[...] = bonus

    ext = jnp.concatenate([ucarry_s[...], u], axis=0)
    ucarry_s[...] = u[tm - POOL_HALO:, :]
    pos = t * tm + lax.broadcasted_iota(jnp.int32, (tm, LANES), 0)
    for gi, win in enumerate(POOL_WINDOWS):
        s = ext[:, gi * LANES:(gi + 1) * LANES]
        for step in range(gi + 1):
            s = s + pltpu.roll(s, 2 ** step, 0)
        count = jnp.minimum(pos + 1, win).astype(F32)
        u_g = u[:, gi * LANES:(gi + 1) * LANES]
        pooled = s[POOL_HALO:, :] / count - u_g
        ypool_o[:, gi * LANES:(gi + 1) * LANES] = _bf(
            _dotb(pooled, poolw_ref[gi]) * pscale_ref[:, gi * LANES:(gi + 1) * LANES])


def _ev_front(x2, bsz, seq, win, mu, w0, w2p, a0, a2p, g2, kkw, ka, rk, ones_bd, poolw, pscale):
    tm = min(ROW_TILE, seq)
    nt = seq // tm
    n = bsz * seq
    row_spec = lambda w: pl.BlockSpec((tm, w), lambda b, t: (b * nt + t, 0))
    consts = (win, mu, w0, w2p, a0, a2p, g2, kkw, ka, rk, ones_bd, poolw, pscale)
    return pl.pallas_call(
        _ev_front_kernel,
        grid=(bsz, nt),
        in_specs=[row_spec(D_MODEL)] + [_full(c.shape) for c in consts],
        out_specs=[row_spec(HALF)] * 9,
        out_shape=[jax.ShapeDtypeStruct((n, HALF), F32)] * 8 + [jax.ShapeDtypeStruct((n, HALF), BF16)],
        scratch_shapes=[pltpu.VMEM((8, RWKV_PROJ), F32), pltpu.VMEM((POOL_HALO, HALF), F32)],
        compiler_params=_params(("arbitrary", "arbitrary")),
        name="ev_front",
    )(x2, *consts)


def _scan_masks(c):
    n = 2 * c
    i = np.arange(n)[:, None]
    j = np.arange(n)[None, :]
    same = (i // c) == (j // c)
    masks = [same & (i > j), same & (i >= j), (i == j), (i // 8 == j // 8) & (i > j)]
    b = 8
    while b < c:
        masks.append((i // (2 * b) == j // (2 * b)) & (i // b != j // b) & (i > j))
        b *= 2
    return np.stack(masks).astype(np.float32)


def _rwkv_scan_kernel(r_ref, lw_ref, k_ref, v_ref, kk_ref, a_ref, tri_ref, masks_ref, y_o, h_s):
    c = SCAN_CHUNK
    rows = r_ref.shape[0]
    n_chunks = rows // c
    n = 2 * c

    @pl.when(pl.program_id(1) == 0)
    def _():
        h_s[...] = jnp.zeros_like(h_s)

    lw = lw_ref[...]
    cl = _split_dot_lhs(tri_ref[...], lw)
    cl_last_rows = [cl[(g + 1) * c - 1:(g + 1) * c, :] for g in range(n_chunks)]
    cl_last = jnp.concatenate([jnp.broadcast_to(m, (c, HALF)) for m in cl_last_rows], axis=0)
    e_in = jnp.exp(cl)
    e_ex = jnp.exp(cl - lw)
    e_neg = jnp.exp(-cl)
    e_rem = jnp.exp(cl_last - cl)

    kk = kk_ref[...]
    kv = k_ref[...]
    beta = kk * a_ref[...]
    abar = -kk * e_ex
    rbar = r_ref[...] * e_in
    btil = beta * e_neg
    ktil = kv * e_neg
    bhat = beta * e_rem
    khat = kv * e_rem
    vv = v_ref[...]

    m_strict = masks_ref[0]
    m_incl = masks_ref[1]
    eye = masks_ref[2]
    m_blk = masks_ref[3]
    n_merge = masks_ref.shape[0] - 4

    lane = lax.broadcasted_iota(jnp.int32, (c, PAIR), 1)
    even = lane < HEAD_DIM

    def stack_bd(m):
        return jnp.concatenate([jnp.where(even, m, 0.0), jnp.where(even, 0.0, m)], axis=0)

    def stack_2(m):
        return jnp.concatenate([m, m], axis=0)

    items = [(g, pr) for g in range(n_chunks) for pr in range(N_PAIRS)]

    def cut(m, it):
        g, pr = it
        return m[g * c:(g + 1) * c, pr * PAIR:(pr + 1) * PAIR]

    abar_bd = [stack_bd(cut(abar, it)) for it in items]
    rbar_bd = [stack_bd(cut(rbar, it)) for it in items]
    v_bd16 = [_bf(stack_bd(cut(vv, it))) for it in items]
    bhat_bd16 = [_bf(stack_bd(cut(bhat, it))) for it in items]
    khat_bd16 = [_bf(stack_bd(cut(khat, it))) for it in items]
    sc = [_dot_nt(_bf(jnp.concatenate([abar_bd[i], rbar_bd[i]], axis=0)),
                  _bf(jnp.concatenate([stack_2(cut(btil, it)), stack_2(cut(ktil, it))], axis=0)))
          for i, it in enumerate(items)]
    a_ab = [m[:n, :n] * m_strict for m in sc]
    a_ak16 = [_bf(m[:n, n:] * m_strict) for m in sc]
    a_rb16 = [_bf(m[n:, :n] * m_incl) for m in sc]
    a_rk16 = [_bf(m[n:, n:] * m_incl) for m in sc]

    a_d = [m * m_blk for m in a_ab]
    pw = [_dotb(m, m) for m in a_d]
    tinv = [_dotb(eye + a_d[i], eye + pw[i]) for i in range(len(items))]
    pw = [_dotb(m, m) for m in pw]
    tinv = [_dotb(tinv[i], eye + pw[i]) for i in range(len(items))]
    for lvl in range(n_merge):
        m_off = masks_ref[4 + lvl]
        tinv16 = [_bf(m) for m in tinv]
        at = [_dot(_bf(a_ab[i] * m_off), tinv16[i]) for i in range(len(items))]
        tinv = [tinv[i] + _dot(tinv16[i], _bf(at[i])) for i in range(len(items))]

    akv = [_dot(a_ak16[i], v_bd16[i]) for i in range(len(items))]
    wu16 = [_bf(_dotb(tinv[i], jnp.concatenate([abar_bd[i], akv[i]], axis=1)))
            for i in range(len(items))]
    arb_wu = [_dot(a_rb16[i], wu16[i]) for i in range(len(items))]
    qhat16 = [_bf(rbar_bd[i] + arb_wu[i][:, :PAIR]) for i in range(len(items))]
    y0 = [arb_wu[i][:, PAIR:] + _dot(a_rk16[i], v_bd16[i]) for i in range(len(items))]
    mg = [_dot_tn(bhat_bd16[i], wu16[i]) for i in range(len(items))]
    m_mat16 = [_bf(m[:, :PAIR]) for m in mg]
    g_mat = [mg[i][:, PAIR:] + _dot_tn(khat_bd16[i], v_bd16[i]) for i in range(len(items))]

    h = [h_s[pr] for pr in range(N_PAIRS)]
    for i, (g, pr) in enumerate(items):
        h16 = _bf(h[pr])
        y_bd = _dot(qhat16[i], h16) + y0[i]
        y_o[g * c:(g + 1) * c, pr * PAIR:(pr + 1) * PAIR] = y_bd[:c, :] + y_bd[c:, :]
        p_last = jnp.exp(cl_last_rows[g][:, pr * PAIR:(pr + 1) * PAIR])
        p_col = jnp.sum(eye * p_last, axis=1, keepdims=True)
        h[pr] = p_col * h[pr] + _dot(m_mat16[i], h16) + g_mat[i]
    for pr in range(N_PAIRS):
        h_s[pr] = h[pr]


def _split_dot_lhs(a_bf16, b):
    hi = _bf(b)
    lo = _bf(b - hi.astype(F32))
    return _dot(a_bf16, hi) + _dot(a_bf16, lo)


def _rwkv_scan(r, lw, k, v, kk, a, bsz, seq):
    c = SCAN_CHUNK
    rows = min(SCAN_ROWS, seq)
    nb = seq // rows
    n = bsz * seq
    tri = jnp.asarray(np.kron(np.eye(rows // c), np.tril(np.ones((c, c)))), BF16)
    masks = jnp.asarray(_scan_masks(c))
    row_spec = pl.BlockSpec((rows, HALF), lambda b, t: (b * nb + t, 0))
    return pl.pallas_call(
        _rwkv_scan_kernel,
        grid=(bsz, nb),
        in_specs=[row_spec] * 6 + [_full(tri.shape), _full(masks.shape)],
        out_specs=row_spec,
        out_shape=jax.ShapeDtypeStruct((n, HALF), F32),
        scratch_shapes=[pltpu.VMEM((N_PAIRS, PAIR, PAIR), F32)],
        compiler_params=_params(("arbitrary", "arbitrary")),
        name="rwkv_scan",
    )(r, lw, k, v, kk, a, tri, masks)


def _ev_back_kernel(y_ref, g_ref, bonus_ref, ypool_ref, x_ref, ones_ref, lng_ref, lnb_ref,
                    wout_ref, mg_ref, mb_ref, o_ref):
    hn = _head_norm(y_ref[...], ones_ref[...], RWKV_NORM_EPS)
    y_rwkv = (hn * lng_ref[...] + lnb_ref[...] + bonus_ref[...]) * g_ref[...]
    h = _dot(_bf(y_rwkv), wout_ref[0:HALF, :]) + _dot(ypool_ref[...], wout_ref[HALF:, :])
    o_ref[...] = _layer_norm(ALPHA * x_ref[...] + h, mg_ref[...], mb_ref[...], LN_EPS)


def _ev_back(y, g, bonus, ypool, x2, ones_bd, lng, lnb, wout, mg, mb):
    n = x2.shape[0]
    tm = min(ROW_TILE, n)
    half_spec = pl.BlockSpec((tm, HALF), lambda i: (i, 0))
    full_spec = pl.BlockSpec((tm, D_MODEL), lambda i: (i, 0))
    consts = (ones_bd, lng, lnb, wout, mg, mb)
    return pl.pallas_call(
        _ev_back_kernel,
        grid=(n // tm,),
        in_specs=[half_spec] * 4 + [full_spec] + [_full(c.shape) for c in consts],
        out_specs=full_spec,
        out_shape=jax.ShapeDtypeStruct((n, D_MODEL), F32),
        compiler_params=_params(("parallel",)),
        name="ev_back",
    )(y, g, bonus, ypool, x2, *consts)


MOE_TILE = 512
MOE_BLOCK = 512
SEG_ALIGN = 16
MOE_TILE_PAD = 640
GROUP_WIDTH = EXPERTS_PER_GROUP * D_EXPERT


def _pad_to(v, m):
    return ((v + (m - 1)) // m) * m


def _seg_layout(counts):
    padded = [_pad_to(c, SEG_ALIGN) for c in counts]
    starts = [0]
    for g in range(1, MOE_GROUPS):
        starts.append(starts[-1] + padded[g - 1])
    return padded, starts


def _moe_route_kernel(x_ref, rw2t_ref, rb_ref, tri_ref,
                      xs_hbm, gs_hbm, pos_o, meta_o,
                      xbuf_s, gbuf_s, zx_s, zg_s, cnt_s, pend_s, sem):
    t = pl.program_id(0)
    nt = pl.num_programs(0)
    tm = x_ref.shape[0]
    cap = xs_hbm.shape[0] // MOE_GROUPS
    slot = t % 2

    def seg_copies(sl, src_row, dst_row):
        src_row = pl.multiple_of(src_row, SEG_ALIGN)
        dst_row = pl.multiple_of(dst_row, SEG_ALIGN)
        return (pltpu.make_async_copy(xbuf_s.at[sl, pl.ds(src_row, SEG_ALIGN)],
                                      xs_hbm.at[pl.ds(dst_row, SEG_ALIGN)], sem.at[sl]),
                pltpu.make_async_copy(gbuf_s.at[sl, pl.ds(src_row, SEG_ALIGN)],
                                      gs_hbm.at[pl.ds(dst_row, SEG_ALIGN)], sem.at[sl]))

    def wait_slot(sl):
        def body(i, carry):
            for cp in seg_copies(sl, 0, 0):
                cp.wait()
            return carry
        lax.fori_loop(0, pend_s[sl], body, 0)
        pend_s[sl] = 0

    @pl.when(t == 0)
    def _():
        for g in range(MOE_GROUPS):
            cnt_s[g] = 0
        pend_s[0] = 0
        pend_s[1] = 0
        zx_s[...] = jnp.zeros_like(zx_s)
        zg_s[...] = jnp.zeros_like(zg_s)

    x = x_ref[...]
    xh = _bf(x)
    xl = _bf(x - xh.astype(F32))
    both = _dot_nt(rw2t_ref[...], xh)
    logits = both[:LANES, :] + both[LANES:, :] + _dot_nt(rw2t_ref[:LANES, :], xl) + rb_ref[...]
    row8 = lax.broadcasted_iota(jnp.int32, (SUBLANES, tm), 0).astype(F32)
    gl = jnp.where(row8 < float(MOE_GROUPS), logits[N_EXPERTS:N_EXPERTS + SUBLANES, :], NEG_BIG)
    gmax = jnp.max(gl, axis=0, keepdims=True)
    gidx = jnp.min(jnp.where(gl == gmax, row8, 1e9), axis=0, keepdims=True)
    gden = jnp.sum(jnp.where(row8 < float(MOE_GROUPS), jnp.exp(gl - gmax), 0.0), axis=0, keepdims=True)
    g_w = 1.0 / gden
    el = logits[0:EXPERTS_PER_GROUP, :]
    for g in range(1, MOE_GROUPS):
        el = jnp.where(gidx == float(g), logits[g * EXPERTS_PER_GROUP:(g + 1) * EXPERTS_PER_GROUP, :], el)
    m1 = jnp.max(el, axis=0, keepdims=True)
    i1 = jnp.min(jnp.where(el == m1, row8, 1e9), axis=0, keepdims=True)
    el2 = jnp.where(row8 == i1, NEG_BIG, el)
    m2 = jnp.max(el2, axis=0, keepdims=True)
    i2 = jnp.min(jnp.where(el2 == m2, row8, 1e9), axis=0, keepdims=True)
    e21 = jnp.exp(m2 - m1)
    w_top = g_w / (1.0 + e21)
    w_sec = g_w * e21 / (1.0 + e21)
    gates8 = jnp.where(row8 == i1, w_top, jnp.where(row8 == i2, w_sec, 0.0))

    onehot = jnp.where(row8 == gidx, 1.0, 0.0)
    counts = [jnp.sum(onehot[g:g + 1, :]).astype(jnp.int32) for g in range(MOE_GROUPS)]
    padded, starts = _seg_layout(counts)
    rank = _dot(_bf(onehot), tri_ref[...])
    start_row = jnp.zeros((1, tm), F32)
    for g in range(1, MOE_GROUPS):
        start_row = jnp.where(gidx == float(g), starts[g].astype(F32), start_row)
    pos = jnp.sum(onehot * rank, axis=0, keepdims=True) + start_row
    pos_o[0] = pos
    perm = jnp.where(pos == lax.broadcasted_iota(jnp.int32, (MOE_TILE_PAD, tm), 0).astype(F32),
                     1.0, 0.0).astype(BF16)

    wait_slot(slot)
    xbuf_s[slot] = _dot(perm, xh).astype(BF16)
    g_hi = _bf(gates8).astype(F32)
    zpad = jnp.zeros((LANES - SUBLANES, tm), F32)
    g_cat = _bf(jnp.concatenate([g_hi, zpad, gates8 - g_hi, zpad], axis=0))
    g_srt = _dot_nt(perm, g_cat)
    gbuf_s[slot] = g_srt[:, :LANES] + g_srt[:, LANES:]

    n_issued = 0
    for g in range(MOE_GROUPS):
        dst0 = g * cap + cnt_s[g]
        meta_o[t * 2 * MOE_GROUPS + g] = dst0
        meta_o[t * 2 * MOE_GROUPS + MOE_GROUPS + g] = padded[g]
        nch = padded[g] // SEG_ALIGN

        def issue(i, carry, g=g, dst0=dst0):
            for cp in seg_copies(slot, starts[g] + i * SEG_ALIGN, dst0 + i * SEG_ALIGN):
                cp.start()
            return carry
        lax.fori_loop(0, nch, issue, 0)
        cnt_s[g] = cnt_s[g] + padded[g]
        n_issued = n_issued + nch
    pend_s[slot] = n_issued

    @pl.when(t == nt - 1)
    def _():
        wait_slot(0)
        wait_slot(1)
        for g in range(MOE_GROUPS):
            dst0 = g * cap + cnt_s[g]

            def zero_copies(i, dst0=dst0):
                row = pl.multiple_of(dst0 + i * SEG_ALIGN, SEG_ALIGN)
                return (pltpu.make_async_copy(zx_s, xs_hbm.at[pl.ds(row, SEG_ALIGN)], sem.at[0]),
                        pltpu.make_async_copy(zg_s, gs_hbm.at[pl.ds(row, SEG_ALIGN)], sem.at[0]))

            def zissue(i, carry):
                for cp in zero_copies(i):
                    cp.start()
                return carry

            def zwait(i, carry):
                for cp in zero_copies(i):
                    cp.wait()
                return carry
            lax.fori_loop(0, MOE_BLOCK // SEG_ALIGN, zissue, 0)
            lax.fori_loop(0, MOE_BLOCK // SEG_ALIGN, zwait, 0)


def _moe_route(x2, rw2, rb):
    n = x2.shape[0]
    tm = MOE_TILE
    nt = n // tm
    cap = _pad_to(n + nt * SEG_ALIGN + MOE_BLOCK, MOE_BLOCK)
    tri = jnp.asarray(np.triu(np.ones((tm, tm)), 1), BF16)
    any_spec = pl.BlockSpec(memory_space=pl.ANY)
    return pl.pallas_call(
        _moe_route_kernel,
        grid=(nt,),
        in_specs=[pl.BlockSpec((tm, D_MODEL), lambda i: (i, 0)), _full(rw2.shape),
                  _full(rb.shape), _full(tri.shape)],
        out_specs=[any_spec, any_spec, pl.BlockSpec((1, 1, tm), lambda i: (i, 0, 0)),
                   pl.BlockSpec(memory_space=pltpu.SMEM)],
        out_shape=[jax.ShapeDtypeStruct((MOE_GROUPS * cap, D_MODEL), BF16),
                   jax.ShapeDtypeStruct((MOE_GROUPS * cap, LANES), F32),
                   jax.ShapeDtypeStruct((nt, 1, tm), F32),
                   jax.ShapeDtypeStruct((nt * 2 * MOE_GROUPS,), jnp.int32)],
        scratch_shapes=[pltpu.VMEM((2, MOE_TILE_PAD, D_MODEL), BF16), pltpu.VMEM((2, MOE_TILE_PAD, LANES), F32),
                        pltpu.VMEM((SEG_ALIGN, D_MODEL), BF16), pltpu.VMEM((SEG_ALIGN, LANES), F32),
                        pltpu.SMEM((MOE_GROUPS,), jnp.int32), pltpu.SMEM((2,), jnp.int32),
                        pltpu.SemaphoreType.DMA((2,))],
        compiler_params=_params(("arbitrary",)),
        name="moe_route",
    )(x2, rw2, rb, tri)


def _moe_experts_kernel(brow_ref, bgrp_ref, nval_ref, xs_ref, gs_ref, w1_ref, w3_ref, w2_ref, ex_ref, ys_ref,
                        w1_s, w3_s, w2_s):
    b = pl.program_id(0)
    valid = b < nval_ref[0]
    new_group = bgrp_ref[b] != bgrp_ref[jnp.maximum(b - 1, 0)]

    @pl.when(valid & ((b == 0) | new_group))
    def _():
        for e in range(EXPERTS_PER_GROUP):
            w1_s[:, e * D_EXPERT:(e + 1) * D_EXPERT] = _bf(w1_ref[e])
            w3_s[:, e * D_EXPERT:(e + 1) * D_EXPERT] = _bf(w3_ref[e])
            w2_s[e * D_EXPERT:(e + 1) * D_EXPERT, :] = _bf(w2_ref[e])

    @pl.when(valid)
    def _():
        xb = xs_ref[...]
        h1 = _dot(xb, w1_s[...])
        h3 = _dot(xb, w3_s[...])
        gexp = _dot(_bf(gs_ref[...]), ex_ref[...])
        act = h1 * _sigmoid(h1) * h3 * gexp
        ys_ref[...] = _bf(_dot(_bf(act), w2_s[...]))

    @pl.when(jnp.logical_not(valid))
    def _():
        ys_ref[...] = jnp.zeros_like(ys_ref)


def _moe_experts(xs, gs, brow, bgrp, nval, layer, w1g, w3g, w2g, expand8):
    rows = xs.shape[0]
    nb = brow.shape[0]
    rb = MOE_BLOCK
    trash = rows // rb

    def out_map(b, brow_ref, bgrp_ref, nval_ref):
        return (jnp.where(b < nval_ref[0], brow_ref[b], trash), 0)

    grid_spec = pltpu.PrefetchScalarGridSpec(
        num_scalar_prefetch=3,
        grid=(nb,),
        in_specs=[pl.BlockSpec((rb, D_MODEL), lambda b, br, bg, nv: (br[b], 0)),
                  pl.BlockSpec((rb, LANES), lambda b, br, bg, nv: (br[b], 0)),
                  pl.BlockSpec((None, EXPERTS_PER_GROUP, D_MODEL, D_EXPERT),
                               lambda b, br, bg, nv: (layer, bg[b], 0, 0)),
                  pl.BlockSpec((None, EXPERTS_PER_GROUP, D_MODEL, D_EXPERT),
                               lambda b, br, bg, nv: (layer, bg[b], 0, 0)),
                  pl.BlockSpec((None, EXPERTS_PER_GROUP, D_EXPERT, D_MODEL),
                               lambda b, br, bg, nv: (layer, bg[b], 0, 0)),
                  pl.BlockSpec(expand8.shape, lambda b, br, bg, nv: (0, 0))],
        out_specs=pl.BlockSpec((rb, D_MODEL), out_map),
        scratch_shapes=[pltpu.VMEM((D_MODEL, GROUP_WIDTH), BF16), pltpu.VMEM((D_MODEL, GROUP_WIDTH), BF16),
                        pltpu.VMEM((GROUP_WIDTH, D_MODEL), BF16)],
    )
    return pl.pallas_call(
        _moe_experts_kernel,
        grid_spec=grid_spec,
        out_shape=jax.ShapeDtypeStruct((rows + rb, D_MODEL), BF16),
        compiler_params=_params(("arbitrary",)),
        name="moe_experts",
    )(brow, bgrp, nval, xs, gs, w1g, w3g, w2g, expand8)


def _moe_combine_kernel(meta_ref, ys_hbm, x_ref, pos_ref, fg_ref, fb_ref, o_ref, ybuf_s, sem):
    t = pl.program_id(0)
    nt = pl.num_programs(0)
    tm = x_ref.shape[0]
    slot = t % 2

    def tile_segments(tt):
        dst = [meta_ref[tt * 2 * MOE_GROUPS + g] for g in range(MOE_GROUPS)]
        padded = [meta_ref[tt * 2 * MOE_GROUPS + MOE_GROUPS + g] for g in range(MOE_GROUPS)]
        starts = [0]
        for g in range(1, MOE_GROUPS):
            starts.append(starts[-1] + padded[g - 1])
        return dst, padded, starts

    def seg_copy(sl, src_row, dst_row):
        src_row = pl.multiple_of(src_row, SEG_ALIGN)
        dst_row = pl.multiple_of(dst_row, SEG_ALIGN)
        return pltpu.make_async_copy(ys_hbm.at[pl.ds(src_row, SEG_ALIGN)],
                                     ybuf_s.at[sl, pl.ds(dst_row, SEG_ALIGN)], sem.at[sl])

    def fetch(tt, sl):
        dst, padded, starts = tile_segments(tt)
        for g in range(MOE_GROUPS):
            def body(i, carry, g=g):
                seg_copy(sl, dst[g] + i * SEG_ALIGN, starts[g] + i * SEG_ALIGN).start()
                return carry
            lax.fori_loop(0, padded[g] // SEG_ALIGN, body, 0)

    def wait_tile(tt, sl):
        _, padded, _ = tile_segments(tt)
        total = padded[0] + padded[1] + padded[2] + padded[3]

        def body(i, carry):
            seg_copy(sl, 0, 0).wait()
            return carry
        lax.fori_loop(0, total // SEG_ALIGN, body, 0)

    @pl.when(t == 0)
    def _():
        ybuf_s[...] = jnp.zeros_like(ybuf_s)
        fetch(0, 0)

    @pl.when(t + 1 < nt)
    def _():
        fetch(t + 1, 1 - slot)

    wait_tile(t, slot)
    perm = jnp.where(pos_ref[0] == lax.broadcasted_iota(jnp.int32, (MOE_TILE_PAD, tm), 0).astype(F32),
                     1.0, 0.0).astype(BF16)
    f = _dot_tn(perm, ybuf_s[slot])
    o_ref[...] = _layer_norm(ALPHA * x_ref[...] + f, fg_ref[...], fb_ref[...], LN_EPS)


def _moe_combine(ys, x2, pos, meta, fg, fb):
    n = x2.shape[0]
    tm = MOE_TILE
    grid_spec = pltpu.PrefetchScalarGridSpec(
        num_scalar_prefetch=1,
        grid=(n // tm,),
        in_specs=[pl.BlockSpec(memory_space=pl.ANY),
                  pl.BlockSpec((tm, D_MODEL), lambda i, m: (i, 0)),
                  pl.BlockSpec((1, 1, tm), lambda i, m: (i, 0, 0)),
                  pl.BlockSpec(fg.shape, lambda i, m: (0, 0)),
                  pl.BlockSpec(fb.shape, lambda i, m: (0, 0))],
        out_specs=pl.BlockSpec((tm, D_MODEL), lambda i, m: (i, 0)),
        scratch_shapes=[pltpu.VMEM((2, MOE_TILE_PAD, D_MODEL), BF16), pltpu.SemaphoreType.DMA((2,))],
    )
    return pl.pallas_call(
        _moe_combine_kernel,
        grid_spec=grid_spec,
        out_shape=jax.ShapeDtypeStruct((n, D_MODEL), F32),
        compiler_params=_params(("arbitrary",)),
        name="moe_combine",
    )(meta, ys, x2, pos, fg, fb)


def _moe(x2, layer, rw2t, rb, e_w1, e_w3, e_w2, expand8, fg, fb):
    n = x2.shape[0]
    xs, gs, pos, meta = _moe_route(x2, rw2t, rb)
    cap = xs.shape[0] // MOE_GROUPS
    last = meta[-2 * MOE_GROUPS:]
    group_rows = last[:MOE_GROUPS] + last[MOE_GROUPS:] - jnp.arange(MOE_GROUPS, dtype=jnp.int32) * cap
    nblk = (group_rows + (MOE_BLOCK - 1)) // MOE_BLOCK
    ends = jnp.cumsum(nblk)
    max_rows = n + (n // MOE_TILE) * MOE_GROUPS * (SEG_ALIGN - 1)
    nb_max = max_rows // MOE_BLOCK + MOE_GROUPS + 1
    b = jnp.arange(nb_max, dtype=jnp.int32)
    bgrp = jnp.minimum(jnp.sum((b[:, None] >= ends[None, :]).astype(jnp.int32), axis=1), MOE_GROUPS - 1)
    brow = bgrp * (cap // MOE_BLOCK) + (b - (ends - nblk)[bgrp])
    nval = ends[-1:]
    last_valid = jnp.maximum(nval[0] - 1, 0)
    brow = jnp.where(b < nval[0], brow, brow[last_valid]).astype(jnp.int32)
    bgrp = jnp.where(b < nval[0], bgrp, bgrp[last_valid]).astype(jnp.int32)
    ys = _moe_experts(xs, gs, brow, bgrp, nval.astype(jnp.int32), layer, e_w1, e_w3, e_w2, expand8)
    return _moe_combine(ys, x2, pos, meta, fg, fb)


def _rope_table_kernel(cos_o, sin_o):
    tm = cos_o.shape[0]
    half = HEAD_DIM // 2
    pos = (pl.program_id(0) * tm + lax.broadcasted_iota(jnp.int32, (tm, LANES), 0)).astype(F32)
    lane = lax.broadcasted_iota(jnp.int32, (tm, LANES), 1)
    idx = (lane & (half - 1)).astype(F32)
    inv = jnp.exp(idx * (-math.log(ROPE_BASE) / half))
    ang = pos * inv
    first = (lane & half) == 0
    c = jnp.cos(ang)
    s = jnp.sin(ang)
    s = jnp.where(first, -s, s)
    for q in range(HALF // LANES):
        cos_o[:, q * LANES:(q + 1) * LANES] = c
        sin_o[:, q * LANES:(q + 1) * LANES] = s


def _rope_table(seq):
    tm = min(ROW_TILE, seq)
    spec = pl.BlockSpec((tm, HALF), lambda i: (i, 0))
    return pl.pallas_call(
        _rope_table_kernel,
        grid=(seq // tm,),
        in_specs=[],
        out_specs=[spec, spec],
        out_shape=[jax.ShapeDtypeStruct((seq, HALF), F32)] * 2,
        compiler_params=_params(("parallel",)),
        name="rope_table",
    )()


def _od_front_kernel(x_ref, win_ref, cw_ref, cb_ref, clg_ref, clb_ref, cos_ref, sin_ref,
                     yconv_o, q_o, k_o, v_o, sg_o, ubuf_s):
    t = pl.program_id(1)
    tm = x_ref.shape[0]

    @pl.when(t == 0)
    def _():
        ubuf_s[0:CONV_HALO, :] = jnp.zeros((CONV_HALO, HALF), F32)

    p = _dot(_bf(x_ref[...]), win_ref[...])
    ca = p[:, 0:HALF]
    cb = p[:, HALF:2 * HALF]
    q = p[:, 2 * HALF:3 * HALF]
    k = p[:, 3 * HALF:4 * HALF]
    v = p[:, 4 * HALF:5 * HALF]
    gr = p[:, 5 * HALF:6 * HALF]

    ubuf_s[CONV_HALO:, :] = ca * _sigmoid(cb)
    acc = jnp.zeros((tm, HALF), F32) + cb_ref[...]
    for rho in range(SUBLANES):
        ext = 0 if rho == 0 else SUBLANES
        part = None
        for j in range(CONV_WIDTH):
            off = CONV_HALO - (CONV_WIDTH - 1) + j
            if off % SUBLANES != rho:
                continue
            base = off - rho
            term = cw_ref[j:j + 1, :] * ubuf_s[base:base + tm + ext, :]
            part = term if part is None else part + term
        if rho == 0:
            acc = acc + part
        else:
            acc = acc + pltpu.roll(part, tm + ext - rho, 0)[0:tm, :]
    ubuf_s[0:CONV_HALO, :] = ubuf_s[tm:tm + CONV_HALO, :]
    ln = _layer_norm(acc, clg_ref[...], clb_ref[...], LN_EPS)
    yconv_o[...] = _bf(ln * _sigmoid(ln))

    lane = lax.broadcasted_iota(jnp.int32, (tm, HALF), 1)
    first = (lane & (HEAD_DIM // 2)) == 0
    cos = cos_ref[...]
    sin = sin_ref[...]

    def rot(m):
        partner = jnp.where(first, pltpu.roll(m, HALF - HEAD_DIM // 2, 1), pltpu.roll(m, HEAD_DIM // 2, 1))
        return m * cos + partner * sin

    q_o[...] = _bf(rot(q))
    k_o[...] = _bf(rot(k) * (HEAD_DIM ** -0.5))
    v_o[...] = _bf(v)
    sg_o[...] = gr * _sigmoid(gr)


def _od_front(x2, bsz, seq, win, cw, cb, clg, clb, cos_t, sin_t):
    tm = min(ROW_TILE, seq)
    nt = seq // tm
    n = bsz * seq
    row_spec = lambda w: pl.BlockSpec((tm, w), lambda b, t: (b * nt + t, 0))
    tab_spec = pl.BlockSpec((tm, HALF), lambda b, t: (t, 0))
    consts = (win, cw, cb, clg, clb)
    sds = lambda dt: jax.ShapeDtypeStruct((n, HALF), dt)
    return pl.pallas_call(
        _od_front_kernel,
        grid=(bsz, nt),
        in_specs=[row_spec(D_MODEL)] + [_full(c.shape) for c in consts] + [tab_spec, tab_spec],
        out_specs=[row_spec(HALF)] * 5,
        out_shape=[sds(BF16), sds(BF16), sds(BF16), sds(BF16), sds(F32)],
        scratch_shapes=[pltpu.VMEM((tm + CONV_HALO, HALF), F32)],
        compiler_params=_params(("arbitrary", "arbitrary")),
        name="od_front",
    )(x2, *consts, cos_t, sin_t)


def _ret_consts(c):
    h = np.arange(N_HEADS, dtype=np.float64)
    log_gamma = np.log1p(-np.power(2.0, -5.0 - h))
    idx = np.arange(c, dtype=np.float64)
    diff = idx[:, None] - idx[None, :]
    dmask = np.where(diff >= 0, np.exp(np.maximum(diff, 0.0)[None] * log_gamma[:, None, None]), 0.0)
    xi = np.exp((idx + 1.0)[:, None] * log_gamma[None, :])
    zeta = np.exp((c - 1.0 - idx)[:, None] * log_gamma[None, :])
    xi = np.repeat(xi, HEAD_DIM, axis=1)
    zeta = np.repeat(zeta, HEAD_DIM, axis=1)
    gamma_c = np.exp(c * log_gamma)
    lane_head = np.arange(PAIR) // HEAD_DIM
    gdiag = np.zeros((N_PAIRS, PAIR, PAIR))
    for pr in range(N_PAIRS):
        same = lane_head[:, None] == lane_head[None, :]
        gdiag[pr] = np.where(same, gamma_c[2 * pr + lane_head][:, None], 0.0)
    bd = (lane_head[:, None] == lane_head[None, :]).astype(np.float32)
    return (dmask.astype(np.float32), xi.astype(np.float32), zeta.astype(np.float32),
            gdiag.astype(np.float32), bd)


def _retention_kernel(q_ref, k_ref, v_ref, sg_ref, dmask_ref, xi_ref, zeta_ref, gdiag_ref, bd_ref,
                      ones_ref, gng_ref, gnb_ref, o_ref, r_s):
    c = q_ref.shape[0]

    @pl.when(pl.program_id(1) == 0)
    def _():
        r_s[...] = jnp.zeros_like(r_s)

    lane = lax.broadcasted_iota(jnp.int32, (c, PAIR), 1)
    even = lane < HEAD_DIM
    bd = bd_ref[...]
    pairs = range(N_PAIRS)
    sls = [slice(pr * PAIR, (pr + 1) * PAIR) for pr in pairs]
    qp = [q_ref[:, sl] for sl in sls]
    kp = [k_ref[:, sl] for sl in sls]
    vp = [v_ref[:, sl] for sl in sls]
    zero = jnp.zeros_like(qp[0])
    s_even = [_bf(_dot_nt(jnp.where(even, qp[pr], zero), kp[pr]) * dmask_ref[2 * pr]) for pr in pairs]
    s_odd = [_bf(_dot_nt(jnp.where(even, zero, qp[pr]), kp[pr]) * dmask_ref[2 * pr + 1]) for pr in pairs]
    r0 = [r_s[pr] for pr in pairs]
    cross = [_dot(_bf(qp[pr].astype(F32) * xi_ref[:, sls[pr]]), _bf(r0[pr])) for pr in pairs]
    kz = [_bf(kp[pr].astype(F32) * zeta_ref[:, sls[pr]]) for pr in pairs]
    kv = [_dot_tn(kz[pr], vp[pr]) for pr in pairs]
    intra = [jnp.where(even, _dot(s_even[pr], vp[pr]), _dot(s_odd[pr], vp[pr])) for pr in pairs]
    ones_pair = ones_ref[...]
    for pr in pairs:
        r_s[pr] = gdiag_ref[pr] * r0[pr] + bd * kv[pr]
        ret = _head_norm(intra[pr] + cross[pr], ones_pair, LN_EPS)
        o_ref[:, sls[pr]] = _bf(sg_ref[:, sls[pr]] * (ret * gng_ref[:, sls[pr]] + gnb_ref[:, sls[pr]]))


def _retention(q, k, v, sg, bsz, seq, gng, gnb):
    c = min(RET_CHUNK, seq)
    nc = seq // c
    n = bsz * seq
    dmask, xi, zeta, gdiag, bd = (jnp.asarray(m) for m in _ret_consts(c))
    ones_pair = jnp.asarray(np.kron(np.eye(2), np.ones((HEAD_DIM, HEAD_DIM))), BF16)
    row_spec = pl.BlockSpec((c, HALF), lambda b, t: (b * nc + t, 0))
    consts = (dmask, xi, zeta, gdiag, bd, ones_pair, gng, gnb)
    return pl.pallas_call(
        _retention_kernel,
        grid=(bsz, nc),
        in_specs=[row_spec] * 4 + [_full(m.shape) for m in consts],
        out_specs=row_spec,
        out_shape=jax.ShapeDtypeStruct((n, HALF), BF16),
        scratch_shapes=[pltpu.VMEM((N_PAIRS, PAIR, PAIR), F32)],
        compiler_params=_params(("arbitrary", "arbitrary")),
        name="retention",
    )(q, k, v, sg, *consts)


def _od_back_kernel(ya_ref, yb_ref, x_ref, wout_ref, mg_ref, mb_ref, o_ref):
    h = _dot(ya_ref[...], wout_ref[0:HALF, :]) + _dot(yb_ref[...], wout_ref[HALF:, :])
    o_ref[...] = _layer_norm(ALPHA * x_ref[...] + h, mg_ref[...], mb_ref[...], LN_EPS)


def _od_back(ya, yb, x2, wout, mg, mb):
    n = x2.shape[0]
    tm = min(ROW_TILE, n)
    half_spec = pl.BlockSpec((tm, HALF), lambda i: (i, 0))
    full_spec = pl.BlockSpec((tm, D_MODEL), lambda i: (i, 0))
    consts = (wout, mg, mb)
    return pl.pallas_call(
        _od_back_kernel,
        grid=(n // tm,),
        in_specs=[half_spec, half_spec, full_spec] + [_full(c.shape) for c in consts],
        out_specs=full_spec,
        out_shape=jax.ShapeDtypeStruct((n, D_MODEL), F32),
        compiler_params=_params(("parallel",)),
        name="od_back",
    )(ya, yb, x2, *consts)


def _row(v):
    return v.reshape(1, -1).astype(F32)


def _moe_weights(rg_w, rg_b, re_w, re_b):
    pad = LANES - N_EXPERTS - MOE_GROUPS
    rw = jnp.concatenate([re_w, rg_w, jnp.zeros((D_MODEL, pad), F32)], axis=1)
    rwh = _bf(rw)
    rwl = _bf(rw - rwh.astype(F32))
    rb = jnp.concatenate([re_b, rg_b, jnp.zeros((pad,), F32)]).reshape(LANES, 1)
    rw2t = jnp.concatenate([rwh.T, rwl.T], axis=0)
    return rw2t, rb


def kernel(x, ev_w_in, ev_mu, ev_w0, ev_w2, ev_a0, ev_a2, ev_g2, ev_k_k, ev_k_a, ev_r_k, ev_lnx_g, ev_lnx_b, ev_pool_w, ev_pool_scale, ev_w_out, od_w_in, od_conv_w, od_conv_b, od_cln_g, od_cln_b, od_gn_g, od_gn_b, od_w_out, ln_mix_g, ln_mix_b, rg_w, rg_b, re_w, re_b, e_w1, e_w3, e_w2, ln_ffn_g, ln_ffn_b):
    bsz, seq, _ = x.shape
    x2 = x.reshape(bsz * seq, D_MODEL)
    ones_bd = jnp.asarray(np.kron(np.eye(N_HEADS), np.ones((HEAD_DIM, HEAD_DIM))), BF16)
    expand = jnp.asarray(
        np.pad(np.kron(np.eye(EXPERTS_PER_GROUP), np.ones((1, D_EXPERT))),
               ((0, LANES - EXPERTS_PER_GROUP), (0, 0))), BF16)
    zeros_lora = jnp.zeros((LORA_W, HALF), F32)

    w2p = jnp.concatenate([ev_w2[0], zeros_lora], axis=0)
    a2p = jnp.concatenate([zeros_lora, ev_a2[0]], axis=0)
    r, lw, k, v, kk, a, g, bonus, ypool = _ev_front(
        x2, bsz, seq, _bf(ev_w_in[0]), _row(ev_mu[0]), _row(ev_w0[0]), _bf(w2p), _row(ev_a0[0]), _bf(a2p),
        _bf(ev_g2[0]), _row(ev_k_k[0]), _row(ev_k_a[0]), _row(ev_r_k[0]), ones_bd, _bf(ev_pool_w[0]),
        _row(ev_pool_scale[0]))
    y = _rwkv_scan(r, lw, k, v, kk, a, bsz, seq)
    x2 = _ev_back(y, g, bonus, ypool, x2, ones_bd, _row(ev_lnx_g[0]), _row(ev_lnx_b[0]), _bf(ev_w_out[0]),
                  _row(ln_mix_g[0]), _row(ln_mix_b[0]))
    x2 = _moe(x2, 0, *_moe_weights(rg_w[0], rg_b[0], re_w[0], re_b[0]), e_w1, e_w3, e_w2, expand,
              _row(ln_ffn_g[0]), _row(ln_ffn_b[0]))

    cos_t, sin_t = _rope_table(seq)
    yconv, q, kr, vr, sg = _od_front(x2, bsz, seq, _bf(od_w_in[0]), od_conv_w[0], _row(od_conv_b[0]),
                                     _row(od_cln_g[0]), _row(od_cln_b[0]), cos_t, sin_t)
    yret = _retention(q, kr, vr, sg, bsz, seq, _row(od_gn_g[0]), _row(od_gn_b[0]))
    x2 = _od_back(yconv, yret, x2, _bf(od_w_out[0]), _row(ln_mix_g[1]), _row(ln_mix_b[1]))
    x2 = _moe(x2, 1, *_moe_weights(rg_w[1], rg_b[1], re_w[1], re_b[1]), e_w1, e_w3, e_w2, expand,
              _row(ln_ffn_g[1]), _row(ln_ffn_b[1]))
    return x2.reshape(bsz, seq, D_MODEL)
```

```python
import functools
import math

import numpy as np
import jax
import jax.numpy as jnp
from jax import lax
from jax.experimental import pallas as pl
from jax.experimental.pallas import tpu as pltpu

F32 = jnp.float32
BF16 = jnp.bfloat16

D_MODEL = 1024
HALF = D_MODEL // 2
HEAD_DIM = 64
N_HEADS = HALF // HEAD_DIM
PAIR = 2 * HEAD_DIM
N_PAIRS = N_HEADS // 2
LORA_W = 64
LORA_A = 64
LORA_G = 128
RWKV_PROJ = 3 * HALF + LORA_W + LORA_A + LORA_G
EVEN_PROJ = RWKV_PROJ + HALF
ODD_PROJ = 6 * HALF
RWKV_NORM_EPS = 64e-5
LN_EPS = 1e-5
POOL_WINDOWS = (2, 4, 8, 16)
POOL_HALO = 16
CONV_WIDTH = 31
CONV_HALO = 32
ROPE_BASE = 10000.0
MOE_GROUPS = 4
EXPERTS_PER_GROUP = 8
N_EXPERTS = MOE_GROUPS * EXPERTS_PER_GROUP
D_EXPERT = 128
DEPTH = 2
ALPHA = (2.0 * DEPTH) ** 0.25
LANES = 128
SUBLANES = 8
NEG_BIG = -1e30

ROW_TILE = 512
SCAN_CHUNK = 64
SCAN_ROWS = 256
RET_CHUNK = 256
VMEM_LIMIT = 56 * 1024 * 1024


def _bf(v):
    return v.astype(BF16)


def _dot(a, b):
    return jnp.dot(a, b, preferred_element_type=F32)


def _dot_nt(a, b):
    return lax.dot_general(a, b, (((1,), (1,)), ((), ())), preferred_element_type=F32)


def _dot_tn(a, b):
    return lax.dot_general(a, b, (((0,), (0,)), ((), ())), preferred_element_type=F32)


def _dotb(a, b):
    return _dot(_bf(a), _bf(b))


def _sigmoid(v):
    return 1.0 / (1.0 + jnp.exp(-v))


def _layer_norm(v, g, b, eps):
    mu = jnp.mean(v, axis=-1, keepdims=True)
    d = v - mu
    var = jnp.mean(d * d, axis=-1, keepdims=True)
    return d * lax.rsqrt(var + eps) * g + b


def _head_norm(v, ones_bd, eps):
    mu = _dot(_bf(v), ones_bd) * (1.0 / HEAD_DIM)
    d = v - mu
    var = _dot(_bf(d * d), ones_bd) * (1.0 / HEAD_DIM)
    return d * lax.rsqrt(var + eps)


def _full(shape):
    nd = len(shape)
    return pl.BlockSpec(shape, lambda *_: (0,) * nd)


def _params(sem):
    return pltpu.CompilerParams(dimension_semantics=sem, vmem_limit_bytes=VMEM_LIMIT)


def _ev_front_kernel(x_ref, win_ref, mu_ref, w0_ref, w2p_ref, a0_ref, a2p_ref, g2_ref, kkw_ref,
                     ka_ref, rk_ref, ones_ref, poolw_ref, pscale_ref,
                     r_o, lw_o, k_o, v_o, kk_o, a_o, g_o, bonus_o, ypool_o,
                     prow_s, ucarry_s):
    t = pl.program_id(1)
    tm = x_ref.shape[0]

    @pl.when(t == 0)
    def _():
        prow_s[...] = jnp.zeros_like(prow_s)
        ucarry_s[...] = jnp.zeros_like(ucarry_s)

    p = _dot(_bf(x_ref[...]), win_ref[...])
    pr = p[:, :RWKV_PROJ]
    u = p[:, RWKV_PROJ:]

    row = lax.broadcasted_iota(jnp.int32, (tm, RWKV_PROJ), 0)
    prev = jnp.where(row == 0, prow_s[0:1, :], pltpu.roll(pr, 1, 0))
    prow_s[0:1, :] = pr[tm - 1:tm, :]
    z = pr + mu_ref[...] * (prev - pr)

    r = z[:, 0:HALF]
    k = z[:, HALF:2 * HALF]
    v = z[:, 2 * HALF:3 * HALF]
    zl = z[:, 3 * HALF:3 * HALF + LORA_W + LORA_A]
    zg = z[:, 3 * HALF + LORA_W + LORA_A:RWKV_PROJ]

    yw = w0_ref[...] + _dotb(jnp.tanh(zl), w2p_ref[...])
    lw = -math.exp(-0.5) * _sigmoid(yw)
    a = _sigmoid(a0_ref[...] + _dotb(zl, a2p_ref[...]))
    g = _dotb(_sigmoid(zg), g2_ref[...])

    ones_bd = ones_ref[...]
    kk = k * kkw_ref[...]
    kk = kk * lax.rsqrt(jnp.maximum(_dot(_bf(kk * kk), ones_bd), 1e-24))
    kmod = k * (1.0 + (a - 1.0) * ka_ref[...])
    bonus = _dot(_bf(r * kmod * rk_ref[...]), ones_bd) * v

    r_o[...] = r
    lw_o[...] = lw
    k_o[...] = kmod
    v_o[...] = v
    kk_o[...] = kk
    a_o[...] = a
    g_o[...] = g
    bonus_o[...] = bonus

    ext = jnp.concatenate([ucarry_s[...], u], axis=0)
    ucarry_s[...] = u[tm - POOL_HALO:, :]
    pos = t * tm + lax.broadcasted_iota(jnp.int32, (tm, LANES), 0)
    for gi, win in enumerate(POOL_WINDOWS):
        s = ext[:, gi * LANES:(gi + 1) * LANES]
        for step in range(gi + 1):
            s = s + pltpu.roll(s, 2 ** step, 0)
        count = jnp.minimum(pos + 1, win).astype(F32)
        u_g = u[:, gi * LANES:(gi + 1) * LANES]
        pooled = s[POOL_HALO:, :] / count - u_g
        ypool_o[:, gi * LANES:(gi + 1) * LANES] = _bf(
            _dotb(pooled, poolw_ref[gi]) * pscale_ref[:, gi * LANES:(gi + 1) * LANES])


def _ev_front(x2, bsz, seq, win, mu, w0, w2p, a0, a2p, g2, kkw, ka, rk, ones_bd, poolw, pscale):
    tm = min(ROW_TILE, seq)
    nt = seq // tm
    n = bsz * seq
    row_spec = lambda w: pl.BlockSpec((tm, w), lambda b, t: (b * nt + t, 0))
    consts = (win, mu, w0, w2p, a0, a2p, g2, kkw, ka, rk, ones_bd, poolw, pscale)
    return pl.pallas_call(
        _ev_front_kernel,
        grid=(bsz, nt),
        in_specs=[row_spec(D_MODEL)] + [_full(c.shape) for c in consts],
        out_specs=[row_spec(HALF)] * 9,
        out_shape=[jax.ShapeDtypeStruct((n, HALF), F32)] * 8 + [jax.ShapeDtypeStruct((n, HALF), BF16)],
        scratch_shapes=[pltpu.VMEM((8, RWKV_PROJ), F32), pltpu.VMEM((POOL_HALO, HALF), F32)],
        compiler_params=_params(("arbitrary", "arbitrary")),
        name="ev_front",
    )(x2, *consts)


def _scan_masks(c):
    n = 2 * c
    i = np.arange(n)[:, None]
    j = np.arange(n)[None, :]
    same = (i // c) == (j // c)
    masks = [same & (i > j), same & (i >= j), (i == j), (i // 8 == j // 8) & (i > j)]
    b = 8
    while b < c:
        masks.append((i // (2 * b) == j // (2 * b)) & (i // b != j // b) & (i > j))
        b *= 2
    return np.stack(masks).astype(np.float32)


def _rwkv_scan_kernel(r_ref, lw_ref, k_ref, v_ref, kk_ref, a_ref, tri_ref, masks_ref, y_o, h_s):
    c = SCAN_CHUNK
    rows = r_ref.shape[0]
    n_chunks = rows // c
    n = 2 * c

    @pl.when(pl.program_id(1) == 0)
    def _():
        h_s[...] = jnp.zeros_like(h_s)

    lw = lw_ref[...]
    cl = _split_dot_lhs(tri_ref[...], lw)
    cl_last_rows = [cl[(g + 1) * c - 1:(g + 1) * c, :] for g in range(n_chunks)]
    cl_last = jnp.concatenate([jnp.broadcast_to(m, (c, HALF)) for m in cl_last_rows], axis=0)
    e_in = jnp.exp(cl)
    e_ex = jnp.exp(cl - lw)
    e_neg = jnp.exp(-cl)
    e_rem = jnp.exp(cl_last - cl)

    kk = kk_ref[...]
    kv = k_ref[...]
    beta = kk * a_ref[...]
    abar = -kk * e_ex
    rbar = r_ref[...] * e_in
    btil = beta * e_neg
    ktil = kv * e_neg
    bhat = beta * e_rem
    khat = kv * e_rem
    vv = v_ref[...]

    m_strict = masks_ref[0]
    m_incl = masks_ref[1]
    eye = masks_ref[2]
    m_blk = masks_ref[3]
    n_merge = masks_ref.shape[0] - 4

    lane = lax.broadcasted_iota(jnp.int32, (c, PAIR), 1)
    even = lane < HEAD_DIM

    def stack_bd(m):
        return jnp.concatenate([jnp.where(even, m, 0.0), jnp.where(even, 0.0, m)], axis=0)

    def stack_2(m):
        return jnp.concatenate([m, m], axis=0)

    items = [(g, pr) for g in range(n_chunks) for pr in range(N_PAIRS)]

    def cut(m, it):
        g, pr = it
        return m[g * c:(g + 1) * c, pr * PAIR:(pr + 1) * PAIR]

    abar_bd = [stack_bd(cut(abar, it)) for it in items]
    rbar_bd = [stack_bd(cut(rbar, it)) for it in items]
    v_bd16 = [_bf(stack_bd(cut(vv, it))) for it in items]
    bhat_bd16 = [_bf(stack_bd(cut(bhat, it))) for it in items]
    khat_bd16 = [_bf(stack_bd(cut(khat, it))) for it in items]
    sc = [_dot_nt(_bf(jnp.concatenate([abar_bd[i], rbar_bd[i]], axis=0)),
                  _bf(jnp.concatenate([stack_2(cut(btil, it)), stack_2(cut(ktil, it))], axis=0)))
          for i, it in enumerate(items)]
    a_ab = [m[:n, :n] * m_strict for m in sc]
    a_ak16 = [_bf(m[:n, n:] * m_strict) for m in sc]
    a_rb16 = [_bf(m[n:, :n] * m_incl) for m in sc]
    a_rk16 = [_bf(m[n:, n:] * m_incl) for m in sc]

    a_d = [m * m_blk for m in a_ab]
    pw = [_dotb(m, m) for m in a_d]
    tinv = [_dotb(eye + a_d[i], eye + pw[i]) for i in range(len(items))]
    pw = [_dotb(m, m) for m in pw]
    tinv = [_dotb(tinv[i], eye + pw[i]) for i in range(len(items))]
    for lvl in range(n_merge):
        m_off = masks_ref[4 + lvl]
        tinv16 = [_bf(m) for m in tinv]
        at = [_dot(_bf(a_ab[i] * m_off), tinv16[i]) for i in range(len(items))]
        tinv = [tinv[i] + _dot(tinv16[i], _bf(at[i])) for i in range(len(items))]

    akv = [_dot(a_ak16[i], v_bd16[i]) for i in range(len(items))]
    wu16 = [_bf(_dotb(tinv[i], jnp.concatenate([abar_bd[i], akv[i]], axis=1)))
            for i in range(len(items))]
    arb_wu = [_dot(a_rb16[i], wu16[i]) for i in range(len(items))]
    qhat16 = [_bf(rbar_bd[i] + arb_wu[i][:, :PAIR]) for i in range(len(items))]
    y0 = [arb_wu[i][:, PAIR:] + _dot(a_rk16[i], v_bd16[i]) for i in range(len(items))]
    mg = [_dot_tn(bhat_bd16[i], wu16[i]) for i in range(len(items))]
    m_mat16 = [_bf(m[:, :PAIR]) for m in mg]
    g_mat = [mg[i][:, PAIR:] + _dot_tn(khat_bd16[i], v_bd16[i]) for i in range(len(items))]

    h = [h_s[pr] for pr in range(N_PAIRS)]
    for i, (g, pr) in enumerate(items):
        h16 = _bf(h[pr])
        y_bd = _dot(qhat16[i], h16) + y0[i]
        y_o[g * c:(g + 1) * c, pr * PAIR:(pr + 1) * PAIR] = y_bd[:c, :] + y_bd[c:, :]
        p_last = jnp.exp(cl_last_rows[g][:, pr * PAIR:(pr + 1) * PAIR])
        p_col = jnp.sum(eye * p_last, axis=1, keepdims=True)
        h[pr] = p_col * h[pr] + _dot(m_mat16[i], h16) + g_mat[i]
    for pr in range(N_PAIRS):
        h_s[pr] = h[pr]


def _split_dot_lhs(a_bf16, b):
    hi = _bf(b)
    lo = _bf(b - hi.astype(F32))
    return _dot(a_bf16, hi) + _dot(a_bf16, lo)


def _rwkv_scan(r, lw, k, v, kk, a, bsz, seq):
    c = SCAN_CHUNK
    rows = min(SCAN_ROWS, seq)
    nb = seq // rows
    n = bsz * seq
    tri = jnp.asarray(np.kron(np.eye(rows // c), np.tril(np.ones((c, c)))), BF16)
    masks = jnp.asarray(_scan_masks(c))
    row_spec = pl.BlockSpec((rows, HALF), lambda b, t: (b * nb + t, 0))
    return pl.pallas_call(
        _rwkv_scan_kernel,
        grid=(bsz, nb),
        in_specs=[row_spec] * 6 + [_full(tri.shape), _full(masks.shape)],
        out_specs=row_spec,
        out_shape=jax.ShapeDtypeStruct((n, HALF), F32),
        scratch_shapes=[pltpu.VMEM((N_PAIRS, PAIR, PAIR), F32)],
        compiler_params=_params(("arbitrary", "arbitrary")),
        name="rwkv_scan",
    )(r, lw, k, v, kk, a, tri, masks)


def _ev_back_tile(y_ref, g_ref, bonus_ref, ypool_ref, x_ref, ones_ref, lng_ref, lnb_ref,
                  wout_ref, mg_ref, mb_ref):
    hn = _head_norm(y_ref[...], ones_ref[...], RWKV_NORM_EPS)
    y_rwkv = (hn * lng_ref[...] + lnb_ref[...] + bonus_ref[...]) * g_ref[...]
    h = _dot(_bf(y_rwkv), wout_ref[0:HALF, :]) + _dot(ypool_ref[...], wout_ref[HALF:, :])
    return _layer_norm(ALPHA * x_ref[...] + h, mg_ref[...], mb_ref[...], LN_EPS)


def _ev_back(y, g, bonus, ypool, x2, ones_bd, lng, lnb, wout, mg, mb, rw2t, rb):
    return _back_route("ev_back_route", _ev_back_tile, (y, g, bonus, ypool, x2),
                       (ones_bd, lng, lnb, wout, mg, mb), rw2t, rb)


MOE_TILE = 512
MOE_BLOCK = 512
SEG_ALIGN = 16
MOE_TILE_PAD = 640
GROUP_WIDTH = EXPERTS_PER_GROUP * D_EXPERT


def _pad_to(v, m):
    return ((v + (m - 1)) // m) * m


def _seg_layout(counts):
    padded = [_pad_to(c, SEG_ALIGN) for c in counts]
    starts = [0]
    for g in range(1, MOE_GROUPS):
        starts.append(starts[-1] + padded[g - 1])
    return padded, starts


def _route_compute(x, rw2t_ref, rb_ref, tri_ref):
    tm = x.shape[0]
    xh = _bf(x)
    xl = _bf(x - xh.astype(F32))
    both = _dot_nt(rw2t_ref[...], xh)
    logits = both[:LANES, :] + both[LANES:, :] + _dot_nt(rw2t_ref[:LANES, :], xl) + rb_ref[...]
    row8 = lax.broadcasted_iota(jnp.int32, (SUBLANES, tm), 0).astype(F32)
    gl = jnp.where(row8 < float(MOE_GROUPS), logits[N_EXPERTS:N_EXPERTS + SUBLANES, :], NEG_BIG)
    gmax = jnp.max(gl, axis=0, keepdims=True)
    gidx = jnp.min(jnp.where(gl == gmax, row8, 1e9), axis=0, keepdims=True)
    gden = jnp.sum(jnp.where(row8 < float(MOE_GROUPS), jnp.exp(gl - gmax), 0.0), axis=0, keepdims=True)
    g_w = 1.0 / gden
    el = logits[0:EXPERTS_PER_GROUP, :]
    for g in range(1, MOE_GROUPS):
        el = jnp.where(gidx == float(g), logits[g * EXPERTS_PER_GROUP:(g + 1) * EXPERTS_PER_GROUP, :], el)
    m1 = jnp.max(el, axis=0, keepdims=True)
    i1 = jnp.min(jnp.where(el == m1, row8, 1e9), axis=0, keepdims=True)
    el2 = jnp.where(row8 == i1, NEG_BIG, el)
    m2 = jnp.max(el2, axis=0, keepdims=True)
    i2 = jnp.min(jnp.where(el2 == m2, row8, 1e9), axis=0, keepdims=True)
    e21 = jnp.exp(m2 - m1)
    w_top = g_w / (1.0 + e21)
    w_sec = g_w * e21 / (1.0 + e21)
    gates8 = jnp.where(row8 == i1, w_top, jnp.where(row8 == i2, w_sec, 0.0))

    onehot = jnp.where(row8 == gidx, 1.0, 0.0)
    counts = [jnp.sum(onehot[g:g + 1, :]).astype(jnp.int32) for g in range(MOE_GROUPS)]
    padded, starts = _seg_layout(counts)
    rank = _dot(_bf(onehot), tri_ref[...])
    start_row = jnp.zeros((1, tm), F32)
    for g in range(1, MOE_GROUPS):
        start_row = jnp.where(gidx == float(g), starts[g].astype(F32), start_row)
    pos = jnp.sum(onehot * rank, axis=0, keepdims=True) + start_row
    perm = jnp.where(pos == lax.broadcasted_iota(jnp.int32, (MOE_TILE_PAD, tm), 0).astype(F32),
                     1.0, 0.0).astype(BF16)
    srt_x = _dot(perm, xh).astype(BF16)
    g_hi = _bf(gates8).astype(F32)
    zpad = jnp.zeros((LANES - SUBLANES, tm), F32)
    g_cat = _bf(jnp.concatenate([g_hi, zpad, gates8 - g_hi, zpad], axis=0))
    g_srt = _dot_nt(perm, g_cat)
    return pos, srt_x, g_srt[:, :LANES] + g_srt[:, LANES:], padded, starts


def _route_emit(t, nt, routed, xs_hbm, gs_hbm, pos_o, meta_o, xbuf_s, gbuf_s, zx_s, zg_s, cnt_s, pend_s, sem):
    pos, srt_x, srt_g, padded, starts = routed
    cap = xs_hbm.shape[0] // MOE_GROUPS
    slot = t % 2
    tt = t

    def seg_copies(sl, src_row, dst_row):
        src_row = pl.multiple_of(src_row, SEG_ALIGN)
        dst_row = pl.multiple_of(dst_row, SEG_ALIGN)
        return (pltpu.make_async_copy(xbuf_s.at[sl, pl.ds(src_row, SEG_ALIGN)],
                                      xs_hbm.at[pl.ds(dst_row, SEG_ALIGN)], sem.at[sl]),
                pltpu.make_async_copy(gbuf_s.at[sl, pl.ds(src_row, SEG_ALIGN)],
                                      gs_hbm.at[pl.ds(dst_row, SEG_ALIGN)], sem.at[sl]))

    def wait_slot(sl):
        def body(i, carry):
            for cp in seg_copies(sl, 0, 0):
                cp.wait()
            return carry
        lax.fori_loop(0, pend_s[sl], body, 0)
        pend_s[sl] = 0

    @pl.when(t == 0)
    def _():
        for g in range(MOE_GROUPS):
            cnt_s[g] = 0
        pend_s[0] = 0
        pend_s[1] = 0
        zx_s[...] = jnp.zeros_like(zx_s)
        zg_s[...] = jnp.zeros_like(zg_s)

    pos_o[0] = pos
    wait_slot(slot)
    xbuf_s[slot] = srt_x
    gbuf_s[slot] = srt_g

    n_issued = 0
    for g in range(MOE_GROUPS):
        dst0 = g * cap + cnt_s[g]
        meta_o[tt * 2 * MOE_GROUPS + g] = dst0
        meta_o[tt * 2 * MOE_GROUPS + MOE_GROUPS + g] = padded[g]
        nch = padded[g] // SEG_ALIGN

        def issue(i, carry, g=g, dst0=dst0):
            for cp in seg_copies(slot, starts[g] + i * SEG_ALIGN, dst0 + i * SEG_ALIGN):
                cp.start()
            return carry
        lax.fori_loop(0, nch, issue, 0)
        cnt_s[g] = cnt_s[g] + padded[g]
        n_issued = n_issued + nch
    pend_s[slot] = n_issued

    @pl.when(t == nt - 1)
    def _():
        wait_slot(0)
        wait_slot(1)
        for g in range(MOE_GROUPS):
            dst0 = g * cap + cnt_s[g]

            def zero_copies(i, dst0=dst0):
                row = pl.multiple_of(dst0 + i * SEG_ALIGN, SEG_ALIGN)
                return (pltpu.make_async_copy(zx_s, xs_hbm.at[pl.ds(row, SEG_ALIGN)], sem.at[0]),
                        pltpu.make_async_copy(zg_s, gs_hbm.at[pl.ds(row, SEG_ALIGN)], sem.at[0]))

            def zissue(i, carry):
                for cp in zero_copies(i):
                    cp.start()
                return carry

            def zwait(i, carry):
                for cp in zero_copies(i):
                    cp.wait()
                return carry
            lax.fori_loop(0, MOE_BLOCK // SEG_ALIGN, zissue, 0)
            lax.fori_loop(0, MOE_BLOCK // SEG_ALIGN, zwait, 0)


def _back_route(name, tile_fn, row_ops, consts, rw2t, rb):
    n_in = len(row_ops) + len(consts)
    tm = MOE_TILE
    n = row_ops[0].shape[0]
    nt = n // tm

    def body(*refs):
        ins = refs[:n_in]
        rw2t_ref, rb_ref, tri_ref = refs[n_in:n_in + 3]
        x1_o = refs[n_in + 3]
        x1 = tile_fn(*ins)
        x1_o[...] = x1
        routed = _route_compute(x1, rw2t_ref, rb_ref, tri_ref)
        _route_emit(pl.program_id(0), nt, routed, *refs[n_in + 4:])

    def row_spec(width):
        return pl.BlockSpec((tm, width), lambda i: (i, 0))

    cap = _pad_to(n + nt * SEG_ALIGN + MOE_BLOCK, MOE_BLOCK)
    tri = jnp.asarray(np.triu(np.ones((tm, tm)), 1), BF16)
    any_spec = pl.BlockSpec(memory_space=pl.ANY)
    operands = tuple(row_ops) + tuple(consts)
    return pl.pallas_call(
        body,
        grid=(nt,),
        in_specs=([row_spec(op.shape[1]) for op in row_ops] + [_full(c.shape) for c in consts]
                  + [_full(rw2t.shape), _full(rb.shape), _full(tri.shape)]),
        out_specs=[row_spec(D_MODEL), any_spec, any_spec,
                   pl.BlockSpec((1, 1, tm), lambda i: (i, 0, 0)),
                   pl.BlockSpec(memory_space=pltpu.SMEM)],
        out_shape=[jax.ShapeDtypeStruct((n, D_MODEL), F32),
                   jax.ShapeDtypeStruct((MOE_GROUPS * cap, D_MODEL), BF16),
                   jax.ShapeDtypeStruct((MOE_GROUPS * cap, LANES), F32),
                   jax.ShapeDtypeStruct((nt, 1, tm), F32),
                   jax.ShapeDtypeStruct((nt * 2 * MOE_GROUPS,), jnp.int32)],
        scratch_shapes=[pltpu.VMEM((2, MOE_TILE_PAD, D_MODEL), BF16), pltpu.VMEM((2, MOE_TILE_PAD, LANES), F32),
                        pltpu.VMEM((SEG_ALIGN, D_MODEL), BF16), pltpu.VMEM((SEG_ALIGN, LANES), F32),
                        pltpu.SMEM((MOE_GROUPS,), jnp.int32), pltpu.SMEM((2,), jnp.int32),
                        pltpu.SemaphoreType.DMA((2,))],
        compiler_params=_params(("arbitrary",)),
        name=name,
    )(*operands, rw2t, rb, tri)


def _moe_experts_kernel(brow_ref, bgrp_ref, nval_ref, xs_ref, gs_ref, w1_ref, w3_ref, w2_ref, ex_ref, ys_ref,
                        w1_s, w3_s, w2_s):
    b = pl.program_id(0)
    valid = b < nval_ref[0]
    new_group = bgrp_ref[b] != bgrp_ref[jnp.maximum(b - 1, 0)]

    @pl.when(valid & ((b == 0) | new_group))
    def _():
        for e in range(EXPERTS_PER_GROUP):
            w1_s[:, e * D_EXPERT:(e + 1) * D_EXPERT] = _bf(w1_ref[e])
            w3_s[:, e * D_EXPERT:(e + 1) * D_EXPERT] = _bf(w3_ref[e])
            w2_s[e * D_EXPERT:(e + 1) * D_EXPERT, :] = _bf(w2_ref[e])

    @pl.when(valid)
    def _():
        xb = xs_ref[...]
        h1 = _dot(xb, w1_s[...])
        h3 = _dot(xb, w3_s[...])
        gexp = _dot(_bf(gs_ref[...]), ex_ref[...])
        act = h1 * _sigmoid(h1) * h3 * gexp
        ys_ref[...] = _bf(_dot(_bf(act), w2_s[...]))

    @pl.when(jnp.logical_not(valid))
    def _():
        ys_ref[...] = jnp.zeros_like(ys_ref)


def _moe_experts(xs, gs, brow, bgrp, nval, layer, w1g, w3g, w2g, expand8):
    rows = xs.shape[0]
    nb = brow.shape[0]
    rb = MOE_BLOCK
    trash = rows // rb

    def out_map(b, brow_ref, bgrp_ref, nval_ref):
        return (jnp.where(b < nval_ref[0], brow_ref[b], trash), 0)

    grid_spec = pltpu.PrefetchScalarGridSpec(
        num_scalar_prefetch=3,
        grid=(nb,),
        in_specs=[pl.BlockSpec((rb, D_MODEL), lambda b, br, bg, nv: (br[b], 0)),
                  pl.BlockSpec((rb, LANES), lambda b, br, bg, nv: (br[b], 0)),
                  pl.BlockSpec((None, EXPERTS_PER_GROUP, D_MODEL, D_EXPERT),
                               lambda b, br, bg, nv: (layer, bg[b], 0, 0)),
                  pl.BlockSpec((None, EXPERTS_PER_GROUP, D_MODEL, D_EXPERT),
                               lambda b, br, bg, nv: (layer, bg[b], 0, 0)),
                  pl.BlockSpec((None, EXPERTS_PER_GROUP, D_EXPERT, D_MODEL),
                               lambda b, br, bg, nv: (layer, bg[b], 0, 0)),
                  pl.BlockSpec(expand8.shape, lambda b, br, bg, nv: (0, 0))],
        out_specs=pl.BlockSpec((rb, D_MODEL), out_map),
        scratch_shapes=[pltpu.VMEM((D_MODEL, GROUP_WIDTH), BF16), pltpu.VMEM((D_MODEL, GROUP_WIDTH), BF16),
                        pltpu.VMEM((GROUP_WIDTH, D_MODEL), BF16)],
    )
    return pl.pallas_call(
        _moe_experts_kernel,
        grid_spec=grid_spec,
        out_shape=jax.ShapeDtypeStruct((rows + rb, D_MODEL), BF16),
        compiler_params=_params(("arbitrary",)),
        name="moe_experts",
    )(brow, bgrp, nval, xs, gs, w1g, w3g, w2g, expand8)


def _moe_combine_kernel(meta_ref, ys_hbm, x_ref, pos_ref, fg_ref, fb_ref, o_ref, ybuf_s, sem):
    t = pl.program_id(0)
    nt = pl.num_programs(0)
    tm = x_ref.shape[0]
    slot = t % 2

    def tile_segments(tt):
        dst = [meta_ref[tt * 2 * MOE_GROUPS + g] for g in range(MOE_GROUPS)]
        padded = [meta_ref[tt * 2 * MOE_GROUPS + MOE_GROUPS + g] for g in range(MOE_GROUPS)]
        starts = [0]
        for g in range(1, MOE_GROUPS):
            starts.append(starts[-1] + padded[g - 1])
        return dst, padded, starts

    def seg_copy(sl, src_row, dst_row):
        src_row = pl.multiple_of(src_row, SEG_ALIGN)
        dst_row = pl.multiple_of(dst_row, SEG_ALIGN)
        return pltpu.make_async_copy(ys_hbm.at[pl.ds(src_row, SEG_ALIGN)],
                                     ybuf_s.at[sl, pl.ds(dst_row, SEG_ALIGN)], sem.at[sl])

    def fetch(tt, sl):
        dst, padded, starts = tile_segments(tt)
        for g in range(MOE_GROUPS):
            def body(i, carry, g=g):
                seg_copy(sl, dst[g] + i * SEG_ALIGN, starts[g] + i * SEG_ALIGN).start()
                return carry
            lax.fori_loop(0, padded[g] // SEG_ALIGN, body, 0)

    def wait_tile(tt, sl):
        _, padded, _ = tile_segments(tt)
        total = padded[0] + padded[1] + padded[2] + padded[3]

        def body(i, carry):
            seg_copy(sl, 0, 0).wait()
            return carry
        lax.fori_loop(0, total // SEG_ALIGN, body, 0)

    @pl.when(t == 0)
    def _():
        ybuf_s[...] = jnp.zeros_like(ybuf_s)
        fetch(0, 0)

    @pl.when(t + 1 < nt)
    def _():
        fetch(t + 1, 1 - slot)

    wait_tile(t, slot)
    perm = jnp.where(pos_ref[0] == lax.broadcasted_iota(jnp.int32, (MOE_TILE_PAD, tm), 0).astype(F32),
                     1.0, 0.0).astype(BF16)
    f = _dot_tn(perm, ybuf_s[slot])
    o_ref[...] = _layer_norm(ALPHA * x_ref[...] + f, fg_ref[...], fb_ref[...], LN_EPS)


def _moe_combine(ys, x2, pos, meta, fg, fb):
    n = x2.shape[0]
    tm = MOE_TILE
    grid_spec = pltpu.PrefetchScalarGridSpec(
        num_scalar_prefetch=1,
        grid=(n // tm,),
        in_specs=[pl.BlockSpec(memory_space=pl.ANY),
                  pl.BlockSpec((tm, D_MODEL), lambda i, m: (i, 0)),
                  pl.BlockSpec((1, 1, tm), lambda i, m: (i, 0, 0)),
                  pl.BlockSpec(fg.shape, lambda i, m: (0, 0)),
                  pl.BlockSpec(fb.shape, lambda i, m: (0, 0))],
        out_specs=pl.BlockSpec((tm, D_MODEL), lambda i, m: (i, 0)),
        scratch_shapes=[pltpu.VMEM((2, MOE_TILE_PAD, D_MODEL), BF16), pltpu.SemaphoreType.DMA((2,))],
    )
    return pl.pallas_call(
        _moe_combine_kernel,
        grid_spec=grid_spec,
        out_shape=jax.ShapeDtypeStruct((n, D_MODEL), F32),
        compiler_params=_params(("arbitrary",)),
        name="moe_combine",
    )(meta, ys, x2, pos, fg, fb)


def _moe(routed, layer, e_w1, e_w3, e_w2, expand8, fg, fb):
    x2, xs, gs, pos, meta = routed
    n = x2.shape[0]
    cap = xs.shape[0] // MOE_GROUPS
    last = meta[-2 * MOE_GROUPS:]
    group_rows = last[:MOE_GROUPS] + last[MOE_GROUPS:] - jnp.arange(MOE_GROUPS, dtype=jnp.int32) * cap
    nblk = (group_rows + (MOE_BLOCK - 1)) // MOE_BLOCK
    ends = jnp.cumsum(nblk)
    max_rows = n + (n // MOE_TILE) * MOE_GROUPS * (SEG_ALIGN - 1)
    nb_max = max_rows // MOE_BLOCK + MOE_GROUPS + 1
    b = jnp.arange(nb_max, dtype=jnp.int32)
    bgrp = jnp.minimum(jnp.sum((b[:, None] >= ends[None, :]).astype(jnp.int32), axis=1), MOE_GROUPS - 1)
    brow = bgrp * (cap // MOE_BLOCK) + (b - (ends - nblk)[bgrp])
    nval = ends[-1:]
    last_valid = jnp.maximum(nval[0] - 1, 0)
    brow = jnp.where(b < nval[0], brow, brow[last_valid]).astype(jnp.int32)
    bgrp = jnp.where(b < nval[0], bgrp, bgrp[last_valid]).astype(jnp.int32)
    ys = _moe_experts(xs, gs, brow, bgrp, nval.astype(jnp.int32), layer, e_w1, e_w3, e_w2, expand8)
    return _moe_combine(ys, x2, pos, meta, fg, fb)


def _rope_table_kernel(cos_o, sin_o):
    tm = cos_o.shape[0]
    half = HEAD_DIM // 2
    pos = (pl.program_id(0) * tm + lax.broadcasted_iota(jnp.int32, (tm, LANES), 0)).astype(F32)
    lane = lax.broadcasted_iota(jnp.int32, (tm, LANES), 1)
    idx = (lane & (half - 1)).astype(F32)
    inv = jnp.exp(idx * (-math.log(ROPE_BASE) / half))
    ang = pos * inv
    first = (lane & half) == 0
    c = jnp.cos(ang)
    s = jnp.sin(ang)
    s = jnp.where(first, -s, s)
    for q in range(HALF // LANES):
        cos_o[:, q * LANES:(q + 1) * LANES] = c
        sin_o[:, q * LANES:(q + 1) * LANES] = s


def _rope_table(seq):
    tm = min(ROW_TILE, seq)
    spec = pl.BlockSpec((tm, HALF), lambda i: (i, 0))
    return pl.pallas_call(
        _rope_table_kernel,
        grid=(seq // tm,),
        in_specs=[],
        out_specs=[spec, spec],
        out_shape=[jax.ShapeDtypeStruct((seq, HALF), F32)] * 2,
        compiler_params=_params(("parallel",)),
        name="rope_table",
    )()


def _od_front_kernel(x_ref, win_ref, cw_ref, cb_ref, clg_ref, clb_ref, cos_ref, sin_ref,
                     yconv_o, q_o, k_o, v_o, sg_o, ubuf_s):
    t = pl.program_id(1)
    tm = x_ref.shape[0]

    @pl.when(t == 0)
    def _():
        ubuf_s[0:CONV_HALO, :] = jnp.zeros((CONV_HALO, HALF), F32)

    p = _dot(_bf(x_ref[...]), win_ref[...])
    ca = p[:, 0:HALF]
    cb = p[:, HALF:2 * HALF]
    q = p[:, 2 * HALF:3 * HALF]
    k = p[:, 3 * HALF:4 * HALF]
    v = p[:, 4 * HALF:5 * HALF]
    gr = p[:, 5 * HALF:6 * HALF]

    ubuf_s[CONV_HALO:, :] = ca * _sigmoid(cb)
    acc = jnp.zeros((tm, HALF), F32) + cb_ref[...]
    for rho in range(SUBLANES):
        ext = 0 if rho == 0 else SUBLANES
        part = None
        for j in range(CONV_WIDTH):
            off = CONV_HALO - (CONV_WIDTH - 1) + j
            if off % SUBLANES != rho:
                continue
            base = off - rho
            term = cw_ref[j:j + 1, :] * ubuf_s[base:base + tm + ext, :]
            part = term if part is None else part + term
        if rho == 0:
            acc = acc + part
        else:
            acc = acc + pltpu.roll(part, tm + ext - rho, 0)[0:tm, :]
    ubuf_s[0:CONV_HALO, :] = ubuf_s[tm:tm + CONV_HALO, :]
    ln = _layer_norm(acc, clg_ref[...], clb_ref[...], LN_EPS)
    yconv_o[...] = _bf(ln * _sigmoid(ln))

    lane = lax.broadcasted_iota(jnp.int32, (tm, HALF), 1)
    first = (lane & (HEAD_DIM // 2)) == 0
    cos = cos_ref[...]
    sin = sin_ref[...]

    def rot(m):
        partner = jnp.where(first, pltpu.roll(m, HALF - HEAD_DIM // 2, 1), pltpu.roll(m, HEAD_DIM // 2, 1))
        return m * cos + partner * sin

    q_o[...] = _bf(rot(q))
    k_o[...] = _bf(rot(k) * (HEAD_DIM ** -0.5))
    v_o[...] = _bf(v)
    sg_o[...] = gr * _sigmoid(gr)


def _od_front(x2, bsz, seq, win, cw, cb, clg, clb, cos_t, sin_t):
    tm = min(ROW_TILE, seq)
    nt = seq // tm
    n = bsz * seq
    row_spec = lambda w: pl.BlockSpec((tm, w), lambda b, t: (b * nt + t, 0))
    tab_spec = pl.BlockSpec((tm, HALF), lambda b, t: (t, 0))
    consts = (win, cw, cb, clg, clb)
    sds = lambda dt: jax.ShapeDtypeStruct((n, HALF), dt)
    return pl.pallas_call(
        _od_front_kernel,
        grid=(bsz, nt),
        in_specs=[row_spec(D_MODEL)] + [_full(c.shape) for c in consts] + [tab_spec, tab_spec],
        out_specs=[row_spec(HALF)] * 5,
        out_shape=[sds(BF16), sds(BF16), sds(BF16), sds(BF16), sds(F32)],
        scratch_shapes=[pltpu.VMEM((tm + CONV_HALO, HALF), F32)],
        compiler_params=_params(("arbitrary", "arbitrary")),
        name="od_front",
    )(x2, *consts, cos_t, sin_t)


def _ret_consts(c):
    h = np.arange(N_HEADS, dtype=np.float64)
    log_gamma = np.log1p(-np.power(2.0, -5.0 - h))
    idx = np.arange(c, dtype=np.float64)
    diff = idx[:, None] - idx[None, :]
    dmask = np.where(diff >= 0, np.exp(np.maximum(diff, 0.0)[None] * log_gamma[:, None, None]), 0.0)
    xi = np.exp((idx + 1.0)[:, None] * log_gamma[None, :])
    zeta = np.exp((c - 1.0 - idx)[:, None] * log_gamma[None, :])
    xi = np.repeat(xi, HEAD_DIM, axis=1)
    zeta = np.repeat(zeta, HEAD_DIM, axis=1)
    gamma_c = np.exp(c * log_gamma)
    lane_head = np.arange(PAIR) // HEAD_DIM
    gdiag = np.zeros((N_PAIRS, PAIR, PAIR))
    for pr in range(N_PAIRS):
        same = lane_head[:, None] == lane_head[None, :]
        gdiag[pr] = np.where(same, gamma_c[2 * pr + lane_head][:, None], 0.0)
    bd = (lane_head[:, None] == lane_head[None, :]).astype(np.float32)
    return (dmask.astype(np.float32), xi.astype(np.float32), zeta.astype(np.float32),
            gdiag.astype(np.float32), bd)


def _retention_kernel(q_ref, k_ref, v_ref, sg_ref, dmask_ref, xi_ref, zeta_ref, gdiag_ref, bd_ref,
                      ones_ref, gng_ref, gnb_ref, o_ref, r_s):
    c = q_ref.shape[0]

    @pl.when(pl.program_id(1) == 0)
    def _():
        r_s[...] = jnp.zeros_like(r_s)

    lane = lax.broadcasted_iota(jnp.int32, (c, PAIR), 1)
    even = lane < HEAD_DIM
    bd = bd_ref[...]
    pairs = range(N_PAIRS)
    sls = [slice(pr * PAIR, (pr + 1) * PAIR) for pr in pairs]
    qp = [q_ref[:, sl] for sl in sls]
    kp = [k_ref[:, sl] for sl in sls]
    vp = [v_ref[:, sl] for sl in sls]
    zero = jnp.zeros_like(qp[0])
    s_even = [_bf(_dot_nt(jnp.where(even, qp[pr], zero), kp[pr]) * dmask_ref[2 * pr]) for pr in pairs]
    s_odd = [_bf(_dot_nt(jnp.where(even, zero, qp[pr]), kp[pr]) * dmask_ref[2 * pr + 1]) for pr in pairs]
    r0 = [r_s[pr] for pr in pairs]
    cross = [_dot(_bf(qp[pr].astype(F32) * xi_ref[:, sls[pr]]), _bf(r0[pr])) for pr in pairs]
    kz = [_bf(kp[pr].astype(F32) * zeta_ref[:, sls[pr]]) for pr in pairs]
    kv = [_dot_tn(kz[pr], vp[pr]) for pr in pairs]
    intra = [jnp.where(even, _dot(s_even[pr], vp[pr]), _dot(s_odd[pr], vp[pr])) for pr in pairs]
    ones_pair = ones_ref[...]
    for pr in pairs:
        r_s[pr] = gdiag_ref[pr] * r0[pr] + bd * kv[pr]
        ret = _head_norm(intra[pr] + cross[pr], ones_pair, LN_EPS)
        o_ref[:, sls[pr]] = _bf(sg_ref[:, sls[pr]] * (ret * gng_ref[:, sls[pr]] + gnb_ref[:, sls[pr]]))


def _retention(q, k, v, sg, bsz, seq, gng, gnb):
    c = min(RET_CHUNK, seq)
    nc = seq // c
    n = bsz * seq
    dmask, xi, zeta, gdiag, bd = (jnp.asarray(m) for m in _ret_consts(c))
    ones_pair = jnp.asarray(np.kron(np.eye(2), np.ones((HEAD_DIM, HEAD_DIM))), BF16)
    row_spec = pl.BlockSpec((c, HALF), lambda b, t: (b * nc + t, 0))
    consts = (dmask, xi, zeta, gdiag, bd, ones_pair, gng, gnb)
    return pl.pallas_call(
        _retention_kernel,
        grid=(bsz, nc),
        in_specs=[row_spec] * 4 + [_full(m.shape) for m in consts],
        out_specs=row_spec,
        out_shape=jax.ShapeDtypeStruct((n, HALF), BF16),
        scratch_shapes=[pltpu.VMEM((N_PAIRS, PAIR, PAIR), F32)],
        compiler_params=_params(("arbitrary", "arbitrary")),
        name="retention",
    )(q, k, v, sg, *consts)


def _od_back_tile(ya_ref, yb_ref, x_ref, wout_ref, mg_ref, mb_ref):
    h = _dot(ya_ref[...], wout_ref[0:HALF, :]) + _dot(yb_ref[...], wout_ref[HALF:, :])
    return _layer_norm(ALPHA * x_ref[...] + h, mg_ref[...], mb_ref[...], LN_EPS)


def _od_back(ya, yb, x2, wout, mg, mb, rw2t, rb):
    return _back_route("od_back_route", _od_back_tile, (ya, yb, x2), (wout, mg, mb), rw2t, rb)


def _row(v):
    return v.reshape(1, -1).astype(F32)


def _moe_weights(rg_w, rg_b, re_w, re_b):
    pad = LANES - N_EXPERTS - MOE_GROUPS
    rw = jnp.concatenate([re_w, rg_w, jnp.zeros((D_MODEL, pad), F32)], axis=1)
    rwh = _bf(rw)
    rwl = _bf(rw - rwh.astype(F32))
    rb = jnp.concatenate([re_b, rg_b, jnp.zeros((pad,), F32)]).reshape(LANES, 1)
    rw2t = jnp.concatenate([rwh.T, rwl.T], axis=0)
    return rw2t, rb


def kernel(x, ev_w_in, ev_mu, ev_w0, ev_w2, ev_a0, ev_a2, ev_g2, ev_k_k, ev_k_a, ev_r_k, ev_lnx_g, ev_lnx_b, ev_pool_w, ev_pool_scale, ev_w_out, od_w_in, od_conv_w, od_conv_b, od_cln_g, od_cln_b, od_gn_g, od_gn_b, od_w_out, ln_mix_g, ln_mix_b, rg_w, rg_b, re_w, re_b, e_w1, e_w3, e_w2, ln_ffn_g, ln_ffn_b):
    bsz, seq, _ = x.shape
    x2 = x.reshape(bsz * seq, D_MODEL)
    ones_bd = jnp.asarray(np.kron(np.eye(N_HEADS), np.ones((HEAD_DIM, HEAD_DIM))), BF16)
    expand = jnp.asarray(
        np.pad(np.kron(np.eye(EXPERTS_PER_GROUP), np.ones((1, D_EXPERT))),
               ((0, LANES - EXPERTS_PER_GROUP), (0, 0))), BF16)
    zeros_lora = jnp.zeros((LORA_W, HALF), F32)

    w2p = jnp.concatenate([ev_w2[0], zeros_lora], axis=0)
    a2p = jnp.concatenate([zeros_lora, ev_a2[0]], axis=0)
    r, lw, k, v, kk, a, g, bonus, ypool = _ev_front(
        x2, bsz, seq, _bf(ev_w_in[0]), _row(ev_mu[0]), _row(ev_w0[0]), _bf(w2p), _row(ev_a0[0]), _bf(a2p),
        _bf(ev_g2[0]), _row(ev_k_k[0]), _row(ev_k_a[0]), _row(ev_r_k[0]), ones_bd, _bf(ev_pool_w[0]),
        _row(ev_pool_scale[0]))
    y = _rwkv_scan(r, lw, k, v, kk, a, bsz, seq)
    routed = _ev_back(y, g, bonus, ypool, x2, ones_bd, _row(ev_lnx_g[0]), _row(ev_lnx_b[0]), _bf(ev_w_out[0]),
                      _row(ln_mix_g[0]), _row(ln_mix_b[0]), *_moe_weights(rg_w[0], rg_b[0], re_w[0], re_b[0]))
    x2 = _moe(routed, 0, e_w1, e_w3, e_w2, expand, _row(ln_ffn_g[0]), _row(ln_ffn_b[0]))

    cos_t, sin_t = _rope_table(seq)
    yconv, q, kr, vr, sg = _od_front(x2, bsz, seq, _bf(od_w_in[0]), od_conv_w[0], _row(od_conv_b[0]),
                                     _row(od_cln_g[0]), _row(od_cln_b[0]), cos_t, sin_t)
    yret = _retention(q, kr, vr, sg, bsz, seq, _row(od_gn_g[0]), _row(od_gn_b[0]))
    routed = _od_back(yconv, yret, x2, _bf(od_w_out[0]), _row(ln_mix_g[1]), _row(ln_mix_b[1]),
                      *_moe_weights(rg_w[1], rg_b[1], re_w[1], re_b[1]))
    x2 = _moe(routed, 1, e_w1, e_w3, e_w2, expand, _row(ln_ffn_g[1]), _row(ln_ffn_b[1]))
    return x2.reshape(bsz, seq, D_MODEL)
```
